```python
import jax, jax.numpy as jnp
from jax import lax
import numpy as np

D_MODEL = 2048
BATCH = 2
SEQ = 4096
DEPTH = 4
DEC_BATCH = 8
DEC_SEQ = 1
PAST_LEN = 16384
PAGE_SIZE = 128

N_MIXERS = 2
N_GMLP_LAYERS = (DEPTH + 1) // 2
N_FOX_LAYERS = DEPTH // 2
CHUNK = 128
GMLP_HALF = 3 * D_MODEL // 2
GMLP_GROUP = 128
GMLP_GROUPS = GMLP_HALF // GMLP_GROUP
N_HEADS = 16
HEAD_DIM = D_MODEL // N_HEADS
Q_BLOCK = 128
D_FF = ((8 * D_MODEL // 3 + 127) // 128) * 128
NORM_EPS = 1e-6
LN_EPS = 1e-5
FFN_RESIDUAL = 0.5
FORGET_BIAS_INIT = 3.0

kernel_name = 'gmlp_fox_macaron_decoder_step'


def rms_norm(x, g):
    xf = x.astype(jnp.float32)
    y = xf * lax.rsqrt(jnp.mean(xf * xf, axis=-1, keepdims=True) + NORM_EPS)
    return (y * g.astype(jnp.float32)).astype(x.dtype)


def layer_norm(x, g, b):
    xf = x.astype(jnp.float32)
    mu = jnp.mean(xf, axis=-1, keepdims=True)
    xc = xf - mu
    var = jnp.mean(xc * xc, axis=-1, keepdims=True)
    return (xc * lax.rsqrt(var + LN_EPS) * g.astype(jnp.float32) + b.astype(jnp.float32)).astype(x.dtype)


def swiglu_ffn(x, g, w_up, w_down):
    h = rms_norm(x, g) @ w_up
    a, b = jnp.split(h, 2, axis=-1)
    return (jax.nn.silu(a) * b) @ w_down


def gmlp_mixer(xn, w_in, ln_g, ln_b, w_s, b_s, w_out):
    b, L, _ = xn.shape
    z = jax.nn.gelu(xn @ w_in, approximate=False)
    u, v = jnp.split(z, 2, axis=-1)
    v = layer_norm(v, ln_g, ln_b)
    n_chunks = -(-L // CHUNK)
    pad = n_chunks * CHUNK - L
    vc = jnp.pad(v, ((0, 0), (0, pad), (0, 0))).reshape(b, n_chunks, CHUNK, GMLP_GROUPS, GMLP_GROUP)
    causal = jnp.tril(jnp.ones((CHUNK, CHUNK), dtype=bool))
    ws = jnp.where(causal[None], w_s, 0.0)
    mixed = jnp.einsum('gts,bcsgd->bctgd', ws.astype(vc.dtype), vc) + b_s.T[None, None, :, :, None]
    mixed = mixed.reshape(b, n_chunks * CHUNK, GMLP_HALF)[:, :L]
    return (u * mixed) @ w_out, v


def fox_project(xn, w_in, b_f, q_g, k_g):
    b, L, _ = xn.shape
    h = xn @ w_in
    q, k, v, gate, f_logit = jnp.split(h, [D_MODEL, 2 * D_MODEL, 3 * D_MODEL, 4 * D_MODEL], axis=-1)
    q = rms_norm(q.reshape(b, L, N_HEADS, HEAD_DIM), q_g)
    k = rms_norm(k.reshape(b, L, N_HEADS, HEAD_DIM), k_g)
    v = v.reshape(b, L, N_HEADS, HEAD_DIM)
    log_f = jax.nn.log_sigmoid(f_logit.astype(jnp.float32) + b_f.astype(jnp.float32))
    return q, k, v, jax.nn.sigmoid(gate), log_f


def fox_attend_prompt(q, k, v, log_f):
    b, L = q.shape[:2]
    scale = HEAD_DIM ** -0.5
    cT = jnp.cumsum(log_f, axis=1).transpose(0, 2, 1)
    key_pos = jnp.arange(L)

    def block(i):
        start = i * Q_BLOCK
        qb = lax.dynamic_slice_in_dim(q, start, Q_BLOCK, axis=1)
        cq = lax.dynamic_slice_in_dim(cT, start, Q_BLOCK, axis=2)
        s = jnp.einsum('bqhd,bkhd->bhqk', qb, k, preferred_element_type=jnp.float32) * scale
        s = s + (cq[..., :, None] - cT[..., None, :])
        qpos = start + jnp.arange(Q_BLOCK)
        s = jnp.where(key_pos[None, :] <= qpos[:, None], s, -jnp.inf)
        p = jax.nn.softmax(s, axis=-1)
        return jnp.einsum('bhqk,bkhd->bqhd', p.astype(v.dtype), v)

    out = lax.map(block, jnp.arange(L // Q_BLOCK))
    return out.transpose(1, 0, 2, 3, 4).reshape(b, L, N_HEADS, HEAD_DIM)


def fox_attend_sample(q, k, v, log_f, k_past, v_past, lf_past):
    T = q.shape[1]
    P = k_past.shape[1]
    scale = HEAD_DIM ** -0.5
    c_new = jnp.cumsum(log_f, axis=1).transpose(0, 2, 1)
    c_past = jnp.cumsum(lf_past.astype(jnp.float32), axis=1).transpose(0, 2, 1)
    suffix_past = c_past[..., -1:] - c_past
    bias_past = suffix_past[..., None, :] + c_new[..., :, None]
    bias_new = c_new[..., :, None] - c_new[..., None, :]
    s_past = jnp.einsum('bqhd,bkhd->bhqk', q, k_past, preferred_element_type=jnp.float32) * scale + bias_past
    s_new = jnp.einsum('bqhd,bkhd->bhqk', q, k, preferred_element_type=jnp.float32) * scale + bias_new
    causal = jnp.tril(jnp.ones((T, T), dtype=bool))
    s_new = jnp.where(causal, s_new, -jnp.inf)
    p = jax.nn.softmax(jnp.concatenate([s_past, s_new], axis=-1), axis=-1)
    out = jnp.einsum('bhqk,bkhd->bqhd', p[..., :P].astype(v_past.dtype), v_past)
    return out + jnp.einsum('bhqk,bkhd->bqhd', p[..., P:].astype(v.dtype), v)


def setup_inputs(seed: int = 0) -> dict:
    key = jax.random.key(seed)
    ks = jax.random.split(key, 32)
    n_pages = PAST_LEN // PAGE_SIZE
    n_used = DEC_BATCH * n_pages
    n_pool = n_used + max(1, n_used // 4)
    f32 = jnp.float32

    def w(k, shape, fan_in):
        return jax.random.normal(k, shape, f32) * fan_in ** -0.5

    def gain(k, shape):
        return 1.0 + 0.02 * jax.random.normal(k, shape, f32)

    x_prompt = jax.random.normal(ks[0], (BATCH, SEQ, D_MODEL), f32)
    x_sample = jax.random.normal(ks[1], (DEC_BATCH, DEC_SEQ, D_MODEL), f32)
    cache_k = jax.random.normal(ks[2], (N_FOX_LAYERS, n_pool, PAGE_SIZE, N_HEADS, HEAD_DIM), f32)
    cache_v = jax.random.normal(ks[3], (N_FOX_LAYERS, n_pool, PAGE_SIZE, N_HEADS, HEAD_DIM), f32)
    cache_logf = jax.nn.log_sigmoid(FORGET_BIAS_INIT + jax.random.normal(ks[4], (N_FOX_LAYERS, n_pool, PAGE_SIZE, N_HEADS), f32))
    page_table = jax.random.permutation(ks[5], n_pool)[:n_used].reshape(DEC_BATCH, n_pages).astype(jnp.int32)
    return {
        'x_prompt': x_prompt,
        'x_sample': x_sample,
        'cache_k': cache_k,
        'cache_v': cache_v,
        'cache_logf': cache_logf,
        'page_table': page_table,
        'norm_ffn1': gain(ks[6], (DEPTH, D_MODEL)),
        'ffn1_w_up': w(ks[7], (DEPTH, D_MODEL, 2 * D_FF), D_MODEL),
        'ffn1_w_down': w(ks[8], (DEPTH, D_FF, D_MODEL), D_FF),
        'norm_mix': gain(ks[9], (DEPTH, D_MODEL)),
        'norm_ffn2': gain(ks[10], (DEPTH, D_MODEL)),
        'ffn2_w_up': w(ks[11], (DEPTH, D_MODEL, 2 * D_FF), D_MODEL),
        'ffn2_w_down': w(ks[12], (DEPTH, D_FF, D_MODEL), D_FF),
        'gm_w_in': w(ks[13], (N_GMLP_LAYERS, D_MODEL, 2 * GMLP_HALF), D_MODEL),
        'gm_ln_g': gain(ks[14], (N_GMLP_LAYERS, GMLP_HALF)),
        'gm_ln_b': 0.02 * jax.random.normal(ks[15], (N_GMLP_LAYERS, GMLP_HALF), f32),
        'gm_w_s': w(ks[16], (N_GMLP_LAYERS, GMLP_GROUPS, CHUNK, CHUNK), CHUNK),
        'gm_b_s': 1.0 + 0.1 * jax.random.normal(ks[17], (N_GMLP_LAYERS, GMLP_GROUPS, CHUNK), f32),
        'gm_w_out': w(ks[18], (N_GMLP_LAYERS, GMLP_HALF, D_MODEL), GMLP_HALF),
        'fox_w_in': w(ks[19], (N_FOX_LAYERS, D_MODEL, 4 * D_MODEL + N_HEADS), D_MODEL),
        'fox_b_f': FORGET_BIAS_INIT + 0.5 * jax.random.normal(ks[20], (N_FOX_LAYERS, N_HEADS), f32),
        'fox_q_norm': gain(ks[21], (N_FOX_LAYERS, HEAD_DIM)),
        'fox_k_norm': gain(ks[22], (N_FOX_LAYERS, HEAD_DIM)),
        'fox_w_out': w(ks[23], (N_FOX_LAYERS, D_MODEL, D_MODEL), D_MODEL),
        'norm_final': gain(ks[24], (D_MODEL,)),
    }


def reference(x_prompt, x_sample, cache_k, cache_v, cache_logf, page_table,
              norm_ffn1, ffn1_w_up, ffn1_w_down, norm_mix, norm_ffn2, ffn2_w_up, ffn2_w_down,
              gm_w_in, gm_ln_g, gm_ln_b, gm_w_s, gm_b_s, gm_w_out,
              fox_w_in, fox_b_f, fox_q_norm, fox_k_norm, fox_w_out, norm_final):
    n_seq, n_pages = page_table.shape
    past_len = n_pages * PAGE_SIZE
    xp, xs = x_prompt, x_sample
    bp, lp = xp.shape[:2]
    bs, ls = xs.shape[:2]
    k_p, v_p, lf_p, k_s, v_s, lf_s, gv_s = [], [], [], [], [], [], []
    for i in range(DEPTH):
        xp = xp + FFN_RESIDUAL * swiglu_ffn(xp, norm_ffn1[i], ffn1_w_up[i], ffn1_w_down[i])
        xs = xs + FFN_RESIDUAL * swiglu_ffn(xs, norm_ffn1[i], ffn1_w_up[i], ffn1_w_down[i])
        hp = rms_norm(xp, norm_mix[i])
        hs = rms_norm(xs, norm_mix[i])
        j = i // N_MIXERS
        if i % N_MIXERS == 0:
            mp, _ = gmlp_mixer(hp, gm_w_in[j], gm_ln_g[j], gm_ln_b[j], gm_w_s[j], gm_b_s[j], gm_w_out[j])
            ms, gv = gmlp_mixer(hs, gm_w_in[j], gm_ln_g[j], gm_ln_b[j], gm_w_s[j], gm_b_s[j], gm_w_out[j])
            gv_s.append(gv)
        else:
            qp, kp, vp, gp, lfp = fox_project(hp, fox_w_in[j], fox_b_f[j], fox_q_norm[j], fox_k_norm[j])
            op = fox_attend_prompt(qp, kp, vp, lfp)
            mp = (op.reshape(bp, lp, D_MODEL) * gp) @ fox_w_out[j]
            qs, ks_, vs, gs, lfs = fox_project(hs, fox_w_in[j], fox_b_f[j], fox_q_norm[j], fox_k_norm[j])
            k_past = cache_k[j, page_table].reshape(n_seq, past_len, N_HEADS, HEAD_DIM)
            v_past = cache_v[j, page_table].reshape(n_seq, past_len, N_HEADS, HEAD_DIM)
            lf_past = cache_logf[j, page_table].reshape(n_seq, past_len, N_HEADS)
            os_ = fox_attend_sample(qs, ks_, vs, lfs, k_past, v_past, lf_past)
            ms = (os_.reshape(bs, ls, D_MODEL) * gs) @ fox_w_out[j]
            k_p.append(kp); v_p.append(vp); lf_p.append(lfp)
            k_s.append(ks_); v_s.append(vs); lf_s.append(lfs)
        xp = xp + mp
        xs = xs + ms
        xp = xp + FFN_RESIDUAL * swiglu_ffn(xp, norm_ffn2[i], ffn2_w_up[i], ffn2_w_down[i])
        xs = xs + FFN_RESIDUAL * swiglu_ffn(xs, norm_ffn2[i], ffn2_w_up[i], ffn2_w_down[i])
    y_prompt = rms_norm(xp, norm_final)
    y_sample = rms_norm(xs, norm_final)
    return (y_prompt, y_sample, jnp.stack(k_p), jnp.stack(v_p), jnp.stack(lf_p),
            jnp.stack(k_s), jnp.stack(v_s), jnp.stack(lf_s), jnp.stack(gv_s))
```

```python
import functools

import jax
import jax.numpy as jnp
from jax import lax
from jax.experimental import pallas as pl
from jax.experimental.pallas import tpu as pltpu

f32 = jnp.float32
bf16 = jnp.bfloat16

N_HEADS = 16
HEAD_DIM = 128
CHUNK = 128
GMLP_GROUP = 128
NORM_EPS = 1e-6
LN_EPS = 1e-5
FFN_RESIDUAL = 0.5
LANES = 128

VMEM_LIMIT_BYTES = 56 * 1024 * 1024
HIGHEST = lax.Precision.HIGHEST


def _params(*sem):
    return pltpu.CompilerParams(dimension_semantics=sem, vmem_limit_bytes=VMEM_LIMIT_BYTES)


def _rms_rows(x, g):
    ms = jnp.mean(x * x, axis=-1, keepdims=True)
    return x * lax.rsqrt(ms + NORM_EPS) * g


def _sigmoid(x):
    return 1.0 / (1.0 + jnp.exp(-x))


def _ffn_up_kernel(x_ref, g_ref, wa_ref, wb_ref, o_ref, xn_ref, *, nb, tn, nvalid):
    j = pl.program_id(1)

    @pl.when(j == 0)
    def _():
        xn_ref[...] = _rms_rows(x_ref[...], g_ref[...]).astype(bf16)

    def act(wa, wb):
        xn = xn_ref[...]
        a = jnp.dot(xn, wa.astype(bf16), preferred_element_type=f32)
        b = jnp.dot(xn, wb.astype(bf16), preferred_element_type=f32)
        return (a * _sigmoid(a) * b).astype(o_ref.dtype)

    if nvalid == tn:
        o_ref[...] = act(wa_ref[0], wb_ref[0])
    else:
        @pl.when(j < nb - 1)
        def _():
            o_ref[...] = act(wa_ref[0], wb_ref[0])

        @pl.when(j == nb - 1)
        def _():
            o_ref[:, :nvalid] = act(wa_ref[0, :, :nvalid], wb_ref[0, :, tn - nvalid:])


def _ffn_up(x, g, w_up, layer, *, tm, tn=256):
    m, d = x.shape
    dff = w_up.shape[2] // 2
    nb = pl.cdiv(dff, tn)
    nvalid = dff - (nb - 1) * tn
    kern = functools.partial(_ffn_up_kernel, nb=nb, tn=tn, nvalid=nvalid)
    return pl.pallas_call(
        kern,
        grid=(m // tm, nb),
        in_specs=[
            pl.BlockSpec((tm, d), lambda i, j: (i, 0)),
            pl.BlockSpec((1, d), lambda i, j: (0, 0)),
            pl.BlockSpec((1, d, tn), lambda i, j: (layer, 0, j)),
            pl.BlockSpec((pl.Element(1), pl.Element(d), pl.Element(tn)),
                         lambda i, j: (layer, 0, pl.multiple_of(jnp.minimum(dff + j * tn, 2 * dff - tn), LANES))),
        ],
        out_specs=pl.BlockSpec((tm, tn), lambda i, j: (i, j)),
        out_shape=jax.ShapeDtypeStruct((m, dff), bf16),
        scratch_shapes=[pltpu.VMEM((tm, d), bf16)],
        compiler_params=_params("parallel", "arbitrary"),
        name="ffn_up",
    )(x, g.reshape(1, d), w_up, w_up)


def _mm_res_kernel(h_ref, w_ref, r_ref, o_ref, wb_ref, *, scale):
    @pl.when(pl.program_id(1) == 0)
    def _():
        wb_ref[...] = w_ref[0].astype(bf16)

    acc = jnp.dot(h_ref[...], wb_ref[...], preferred_element_type=f32)
    o_ref[...] = r_ref[...] + scale * acc


def _mm_res(h, w, layer, res, scale, *, tm, tn=512):
    m, k = h.shape
    n = w.shape[2]
    return pl.pallas_call(
        functools.partial(_mm_res_kernel, scale=scale),
        grid=(n // tn, m // tm),
        in_specs=[
            pl.BlockSpec((tm, k), lambda j, i: (i, 0)),
            pl.BlockSpec((1, k, tn), lambda j, i: (layer, 0, j)),
            pl.BlockSpec((tm, tn), lambda j, i: (i, j)),
        ],
        out_specs=pl.BlockSpec((tm, tn), lambda j, i: (i, j)),
        out_shape=jax.ShapeDtypeStruct((m, n), f32),
        scratch_shapes=[pltpu.VMEM((k, tn), bf16)],
        compiler_params=_params("parallel", "arbitrary"),
        name="mm_res",
    )(h, w, res)


def _ffn(x, g, w_up, w_down, layer, *, tm_up, tm_down):
    h = _ffn_up(x, g, w_up, layer, tm=tm_up)
    return _mm_res(h, w_down, layer, x, FFN_RESIDUAL, tm=tm_down)


def _head_rms(acc, gain, tn):
    outs = []
    for hh in range(tn // HEAD_DIM):
        blk = acc[:, hh * HEAD_DIM:(hh + 1) * HEAD_DIM]
        ms = jnp.mean(blk * blk, axis=-1, keepdims=True)
        outs.append(blk * lax.rsqrt(ms + NORM_EPS) * gain)
    return jnp.concatenate(outs, axis=1)


def _rms_proj_kernel(*refs, mode, tn, n_out):
    x_ref, g_ref, w_ref = refs[:3]
    extra = refs[3:-n_out - 1]
    outs = refs[-n_out - 1:-1]
    xn_ref = refs[-1]

    @pl.when(pl.program_id(1) == 0)
    def _():
        xn_ref[...] = _rms_rows(x_ref[...], g_ref[...]).astype(bf16)

    acc = jnp.dot(xn_ref[...], w_ref[0].astype(bf16), preferred_element_type=f32)
    if mode == "gelu":
        outs[0][...] = 0.5 * acc * (1.0 + lax.erf(acc * (0.5 ** 0.5)))
    elif mode == "q":
        outs[0][...] = (_head_rms(acc, extra[0][...], tn) * (HEAD_DIM ** -0.5)).astype(bf16)
    elif mode == "k":
        kn = _head_rms(acc, extra[0][...], tn)
        outs[0][...] = kn
        outs[1][...] = kn.astype(bf16)
    elif mode == "v":
        outs[0][...] = acc
        outs[1][...] = acc.astype(bf16)
    elif mode == "gate":
        outs[0][...] = _sigmoid(acc)
    else:
        raise ValueError(mode)


def _rms_proj(x, g, w, layer, *, col0, ncols, mode, tm, tn=512, gain=None):
    m, d = x.shape
    out_dtypes = {"gelu": (f32,), "q": (bf16,), "k": (f32, bf16), "v": (f32, bf16), "gate": (f32,)}[mode]
    in_specs = [
        pl.BlockSpec((tm, d), lambda i, j: (i, 0)),
        pl.BlockSpec((1, d), lambda i, j: (0, 0)),
        pl.BlockSpec((1, d, tn), lambda i, j: (layer, 0, col0 // tn + j)),
    ]
    args = [x, g.reshape(1, d), w]
    if gain is not None:
        in_specs.append(pl.BlockSpec((1, HEAD_DIM), lambda i, j: (0, 0)))
        args.append(gain.reshape(1, HEAD_DIM))
    out = pl.pallas_call(
        functools.partial(_rms_proj_kernel, mode=mode, tn=tn, n_out=len(out_dtypes)),
        grid=(m // tm, ncols // tn),
        in_specs=in_specs,
        out_specs=[pl.BlockSpec((tm, tn), lambda i, j: (i, j)) for _ in out_dtypes],
        out_shape=[jax.ShapeDtypeStruct((m, ncols), dt) for dt in out_dtypes],
        scratch_shapes=[pltpu.VMEM((tm, d), bf16)],
        compiler_params=_params("parallel", "arbitrary"),
        name="rms_proj_" + mode,
    )(*args)
    return out


def _logf_kernel(x_ref, g_ref, w_ref, b_ref, o_ref):
    xn = _rms_rows(x_ref[...], g_ref[...]).astype(bf16)
    z = jnp.dot(xn, w_ref[...].astype(bf16), preferred_element_type=f32) + b_ref[...]
    o_ref[...] = jnp.minimum(z, 0.0) - jnp.log1p(jnp.exp(-jnp.abs(z)))


def _logf_proj(x, g, w_f, b_f, *, tm):
    m, d = x.shape
    return pl.pallas_call(
        _logf_kernel,
        grid=(m // tm,),
        in_specs=[
            pl.BlockSpec((tm, d), lambda i: (i, 0)),
            pl.BlockSpec((1, d), lambda i: (0, 0)),
            pl.BlockSpec((d, N_HEADS), lambda i: (0, 0)),
            pl.BlockSpec((1, N_HEADS), lambda i: (0, 0)),
        ],
        out_specs=pl.BlockSpec((tm, N_HEADS), lambda i: (i, 0)),
        out_shape=jax.ShapeDtypeStruct((m, N_HEADS), f32),
        compiler_params=_params("parallel"),
        name="logf_proj",
    )(x, g.reshape(1, d), w_f, b_f.reshape(1, N_HEADS))


def _layer_norm_rows(v, g, b):
    mu = jnp.mean(v, axis=-1, keepdims=True)
    vc = v - mu
    var = jnp.mean(vc * vc, axis=-1, keepdims=True)
    return vc * lax.rsqrt(var + LN_EPS) * g + b


def _gmlp_mix_kernel(u_ref, v_ref, g_ref, b_ref, ws_ref, bs_ref, o_ref, *, n_groups, n_chunks):
    vn = _layer_norm_rows(v_ref[...], g_ref[...], b_ref[...]).astype(bf16)
    row = lax.broadcasted_iota(jnp.int32, (CHUNK, CHUNK), 0)
    col = lax.broadcasted_iota(jnp.int32, (CHUNK, CHUNK), 1)
    causal = row >= col
    for gi in range(n_groups):
        cols = slice(gi * GMLP_GROUP, (gi + 1) * GMLP_GROUP)
        ws = jnp.where(causal, ws_ref[gi], 0.0).astype(bf16)
        rhs = jnp.concatenate([vn[c * CHUNK:(c + 1) * CHUNK, cols] for c in range(n_chunks)], axis=1)
        mixed = jnp.dot(ws, rhs, preferred_element_type=f32) + bs_ref[:, gi:gi + 1]
        for c in range(n_chunks):
            rows = slice(c * CHUNK, (c + 1) * CHUNK)
            o_ref[rows, cols] = (u_ref[rows, cols] * mixed[:, c * GMLP_GROUP:(c + 1) * GMLP_GROUP]).astype(bf16)


def _gmlp_mix(z, ln_g, ln_b, w_s, b_s_t, *, n_chunks=2):
    m = z.shape[0]
    half = z.shape[1] // 2
    n_groups = half // GMLP_GROUP
    tm = n_chunks * CHUNK
    return pl.pallas_call(
        functools.partial(_gmlp_mix_kernel, n_groups=n_groups, n_chunks=n_chunks),
        grid=(m // tm,),
        in_specs=[
            pl.BlockSpec((tm, half), lambda i: (i, 0)),
            pl.BlockSpec((tm, half), lambda i: (i, 1)),
            pl.BlockSpec((1, half), lambda i: (0, 0)),
            pl.BlockSpec((1, half), lambda i: (0, 0)),
            pl.BlockSpec((n_groups, CHUNK, CHUNK), lambda i: (0, 0, 0)),
            pl.BlockSpec((CHUNK, n_groups), lambda i: (0, 0)),
        ],
        out_specs=pl.BlockSpec((tm, half), lambda i: (i, 0)),
        out_shape=jax.ShapeDtypeStruct((m, half), bf16),
        compiler_params=_params("parallel"),
        name="gmlp_mix",
    )(z, z, ln_g.reshape(1, half), ln_b.reshape(1, half), w_s, b_s_t)


def _gmlp_mix_first_row_kernel(u_ref, v_ref, g_ref, b_ref, w0_ref, b0_ref, o_ref, vn_ref):
    vn = _layer_norm_rows(v_ref[...], g_ref[...], b_ref[...])
    vn_ref[...] = vn
    mixed = vn * w0_ref[...] + b0_ref[...]
    o_ref[...] = (u_ref[...] * mixed).astype(bf16)


def _gmlp_mix_first_row(z, ln_g, ln_b, w00, b0):
    m = z.shape[0]
    half = z.shape[1] // 2
    row = lambda: pl.BlockSpec((1, half), lambda i: (0, 0))
    return pl.pallas_call(
        _gmlp_mix_first_row_kernel,
        grid=(1,),
        in_specs=[
            pl.BlockSpec((m, half), lambda i: (0, 0)),
            pl.BlockSpec((m, half), lambda i: (0, 1)),
            row(), row(), row(), row(),
        ],
        out_specs=[pl.BlockSpec((m, half), lambda i: (0, 0)), pl.BlockSpec((m, half), lambda i: (0, 0))],
        out_shape=[jax.ShapeDtypeStruct((m, half), bf16), jax.ShapeDtypeStruct((m, half), f32)],
        compiler_params=_params("arbitrary"),
        name="gmlp_mix_first_row",
    )(z, z, ln_g.reshape(1, half), ln_b.reshape(1, half), w00.reshape(1, half), b0.reshape(1, half))


def _cumsum_kernel(x_ref, o_ref, *, n_blocks):
    row = lax.broadcasted_iota(jnp.int32, (CHUNK, CHUNK), 0)
    col = lax.broadcasted_iota(jnp.int32, (CHUNK, CHUNK), 1)
    tri = (row >= col).astype(f32)
    carry = jnp.zeros((1, N_HEADS), f32)
    for blk in range(n_blocks):
        rows = slice(blk * CHUNK, (blk + 1) * CHUNK)
        cs = jnp.dot(tri, x_ref[0, rows, :], precision=HIGHEST, preferred_element_type=f32) + carry
        o_ref[0, rows, :] = cs
        carry = cs[CHUNK - 1:CHUNK, :]


def _cumsum_seq(lf):
    b, l, h = lf.shape
    return pl.pallas_call(
        functools.partial(_cumsum_kernel, n_blocks=l // CHUNK),
        grid=(b,),
        in_specs=[pl.BlockSpec((1, l, h), lambda i: (i, 0, 0))],
        out_specs=pl.BlockSpec((1, l, h), lambda i: (i, 0, 0)),
        out_shape=jax.ShapeDtypeStruct((b, l, h), f32),
        compiler_params=_params("parallel"),
        name="cumsum_seq",
    )(lf)


def _flash_kernel(q_ref, k_ref, v_ref, c_ref, ct_ref, gate_ref, o_ref, *, tq):
    h = pl.program_id(1)
    qi = pl.program_id(2)
    q = q_ref[0]
    lane = lax.broadcasted_iota(jnp.int32, (tq, N_HEADS), 1)
    cq = jnp.sum(jnp.where(lane == h, c_ref[0], 0.0), axis=1, keepdims=True)

    def scores(kj):
        start = pl.multiple_of(kj * tq, tq)
        k = k_ref[0, pl.ds(start, tq), :]
        s = lax.dot_general(q, k, (((1,), (1,)), ((), ())), preferred_element_type=f32)
        return s + (cq - ct_ref[0, 0, :, pl.ds(start, tq)])

    def update(kj, s, carry):
        m, l, acc = carry
        start = pl.multiple_of(kj * tq, tq)
        m_new = jnp.maximum(m, jnp.max(s, axis=1, keepdims=True))
        p = jnp.exp(s - m_new)
        alpha = jnp.exp(m - m_new)
        l = alpha * l + jnp.sum(p, axis=1, keepdims=True)
        pv = jnp.dot(p.astype(bf16), v_ref[0, pl.ds(start, tq), :], preferred_element_type=f32)
        return m_new, l, alpha * acc + pv

    def body(kj, carry):
        return update(kj, scores(kj), carry)

    init = (jnp.full((tq, 1), -jnp.inf, f32), jnp.zeros((tq, 1), f32), jnp.zeros((tq, HEAD_DIM), f32))
    carry = lax.fori_loop(0, qi, body, init)
    row = lax.broadcasted_iota(jnp.int32, (tq, tq), 0)
    col = lax.broadcasted_iota(jnp.int32, (tq, tq), 1)
    m, l, acc = update(qi, jnp.where(col <= row, scores(qi), -jnp.inf), carry)
    o_ref[0] = ((acc / l) * gate_ref[0]).astype(bf16)


def _flash_prompt(q, k, v, c, ct, gate, *, tq=512):
    b, l, d = q.shape
    blk = lambda: pl.BlockSpec((1, tq, HEAD_DIM), lambda bi, h, qi: (bi, qi, h))
    seq = lambda: pl.BlockSpec((1, l, HEAD_DIM), lambda bi, h, qi: (bi, 0, h))
    return pl.pallas_call(
        functools.partial(_flash_kernel, tq=tq),
        grid=(b, N_HEADS, l // tq),
        in_specs=[
            blk(), seq(), seq(),
            pl.BlockSpec((1, tq, N_HEADS), lambda bi, h, qi: (bi, qi, 0)),
            pl.BlockSpec((1, 1, 1, l), lambda bi, h, qi: (bi, h, 0, 0)),
            blk(),
        ],
        out_specs=blk(),
        out_shape=jax.ShapeDtypeStruct((b, l, d), bf16),
        compiler_params=_params("parallel", "parallel", "arbitrary"),
        name="flash_prompt",
    )(q, k, v, c, ct, gate)


def _head_scores(k3, q_pad):
    t = k3.shape[0]
    k2 = k3.reshape(t * N_HEADS, HEAD_DIM).astype(bf16)
    s = lax.dot_general(k2, q_pad, (((1,), (1,)), ((), ())), preferred_element_type=f32)
    s3 = s.reshape(t, N_HEADS, LANES)
    hrow = lax.broadcasted_iota(jnp.int32, (N_HEADS, LANES), 0)
    hcol = lax.broadcasted_iota(jnp.int32, (N_HEADS, LANES), 1)
    diag = jnp.sum(jnp.where((hrow == hcol)[None], s3, 0.0), axis=1)
    return diag[:, :N_HEADS]


def _dec_scores_kernel(pt_ref, q_ref, k_ref, lf_ref, knew_ref, lfnew_ref, s_ref, carry_ref, *, n_pages, page):
    p = pl.program_id(1)

    @pl.when(p == 0)
    def _():
        carry_ref[...] = jnp.zeros_like(carry_ref)

    @pl.when(p < n_pages)
    def _():
        lf = lf_ref[0, 0]
        row = lax.broadcasted_iota(jnp.int32, (page, page), 0)
        col = lax.broadcasted_iota(jnp.int32, (page, page), 1)
        later = (col > row).astype(f32)
        suffix = jnp.dot(later, lf, precision=HIGHEST, preferred_element_type=f32) + carry_ref[...]
        s_ref[0, 0] = _head_scores(k_ref[0, 0], q_ref[0]) + (suffix + lfnew_ref[0])
        carry_ref[...] += jnp.sum(lf, axis=0, keepdims=True)

    @pl.when(p == n_pages)
    def _():
        s_new = _head_scores(knew_ref[...], q_ref[0])
        row = lax.broadcasted_iota(jnp.int32, (page, N_HEADS), 0)
        s_ref[0, 0] = jnp.where(row == 0, s_new, -jnp.inf)


def _dec_scores(page_table, q_pad, cache_k, cache_logf, layer, k_new, lf_new):
    nb, n_pages = page_table.shape
    page = cache_k.shape[2]

    def past(b, p, pt):
        return pt[b, jnp.maximum(n_pages - 1 - p, 0)]

    return pl.pallas_call(
        functools.partial(_dec_scores_kernel, n_pages=n_pages, page=page),
        grid_spec=pltpu.PrefetchScalarGridSpec(
            num_scalar_prefetch=1,
            grid=(nb, n_pages + 1),
            in_specs=[
                pl.BlockSpec((1, LANES, HEAD_DIM), lambda b, p, pt: (b, 0, 0)),
                pl.BlockSpec((1, 1, page, N_HEADS, HEAD_DIM), lambda b, p, pt: (layer, past(b, p, pt), 0, 0, 0)),
                pl.BlockSpec((1, 1, page, N_HEADS), lambda b, p, pt: (layer, past(b, p, pt), 0, 0)),
                pl.BlockSpec((1, N_HEADS, HEAD_DIM), lambda b, p, pt: (b, 0, 0)),
                pl.BlockSpec((1, 1, N_HEADS), lambda b, p, pt: (b, 0, 0)),
            ],
            out_specs=pl.BlockSpec(
                (1, 1, page, N_HEADS),
                lambda b, p, pt: (b, jnp.where(p < n_pages, n_pages - 1 - p, n_pages), 0, 0)),
            scratch_shapes=[pltpu.VMEM((1, N_HEADS), f32)],
        ),
        out_shape=jax.ShapeDtypeStruct((nb, n_pages + 1, page, N_HEADS), f32),
        compiler_params=_params("parallel", "arbitrary"),
        name="dec_scores",
    )(page_table, q_pad, cache_k, cache_logf, k_new, lf_new)


def _weighted_rows(p_rows, v3):
    t = p_rows.shape[0]
    hrow = lax.broadcasted_iota(jnp.int32, (N_HEADS, LANES), 0)
    hcol = lax.broadcasted_iota(jnp.int32, (N_HEADS, LANES), 1)
    spread = jnp.where((hrow == hcol)[None], jnp.broadcast_to(p_rows[:, None, :], (t, N_HEADS, LANES)), 0.0)
    ones = jnp.ones((LANES, HEAD_DIM), bf16)
    pb = jnp.dot(spread.reshape(t * N_HEADS, LANES).astype(bf16), ones, preferred_element_type=f32)
    return jnp.sum(pb.reshape(t, N_HEADS, HEAD_DIM) * v3, axis=0)


def _dec_pv_kernel(pt_ref, s_ref, v_ref, vnew_ref, gate_ref, o_ref, p_ref, acc_ref, *, n_pages):
    p = pl.program_id(1)

    @pl.when(p == 0)
    def _():
        s = s_ref[0]
        m = jnp.max(jnp.max(s, axis=0), axis=0, keepdims=True)
        e = jnp.exp(s - m[None])
        denom = jnp.sum(jnp.sum(e, axis=0), axis=0, keepdims=True)
        p_ref[...] = jnp.zeros_like(p_ref)
        p_ref[:, :, :N_HEADS] = e / denom[None]
        acc_ref[...] = jnp.zeros_like(acc_ref)

    @pl.when(p < n_pages)
    def _():
        acc_ref[...] += _weighted_rows(p_ref[p], v_ref[0, 0])

    @pl.when(p == n_pages)
    def _():
        acc = acc_ref[...] + _weighted_rows(p_ref[n_pages, 0:1, :], vnew_ref[...])
        o_ref[0] = acc * gate_ref[0]


def _dec_pv(page_table, s_all, cache_v, layer, v_new, gate):
    nb, n_pages = page_table.shape
    page = cache_v.shape[2]
    head_blk = lambda: pl.BlockSpec((1, N_HEADS, HEAD_DIM), lambda b, p, pt: (b, 0, 0))
    return pl.pallas_call(
        functools.partial(_dec_pv_kernel, n_pages=n_pages),
        grid_spec=pltpu.PrefetchScalarGridSpec(
            num_scalar_prefetch=1,
            grid=(nb, n_pages + 1),
            in_specs=[
                pl.BlockSpec((1, n_pages + 1, page, N_HEADS), lambda b, p, pt: (b, 0, 0, 0)),
                pl.BlockSpec((1, 1, page, N_HEADS, HEAD_DIM),
                             lambda b, p, pt: (layer, pt[b, jnp.minimum(p, n_pages - 1)], 0, 0, 0)),
                head_blk(), head_blk(),
            ],
            out_specs=head_blk(),
            scratch_shapes=[pltpu.VMEM((n_pages + 1, page, LANES), f32), pltpu.VMEM((N_HEADS, HEAD_DIM), f32)],
        ),
        out_shape=jax.ShapeDtypeStruct((nb, N_HEADS, HEAD_DIM), f32),
        compiler_params=_params("parallel", "arbitrary"),
        name="dec_pv",
    )(page_table, s_all, cache_v, v_new, gate)


def _final_norm_kernel(x_ref, g_ref, o_ref):
    o_ref[...] = _rms_rows(x_ref[...], g_ref[...])


def _final_norm(x, g, *, tm):
    m, d = x.shape
    return pl.pallas_call(
        _final_norm_kernel,
        grid=(m // tm,),
        in_specs=[pl.BlockSpec((tm, d), lambda i: (i, 0)), pl.BlockSpec((1, d), lambda i: (0, 0))],
        out_specs=pl.BlockSpec((tm, d), lambda i: (i, 0)),
        out_shape=jax.ShapeDtypeStruct((m, d), f32),
        compiler_params=_params("parallel"),
        name="final_norm",
    )(x, g.reshape(1, d))


def kernel(x_prompt, x_sample, cache_k, cache_v, cache_logf, page_table, norm_ffn1, ffn1_w_up, ffn1_w_down, norm_mix, norm_ffn2, ffn2_w_up, ffn2_w_down, gm_w_in, gm_ln_g, gm_ln_b, gm_w_s, gm_b_s, gm_w_out, fox_w_in, fox_b_f, fox_q_norm, fox_k_norm, fox_w_out, norm_final):
    bp, lp, d = x_prompt.shape
    bs, ls, _ = x_sample.shape
    assert ls == 1 and lp % 512 == 0 and d == N_HEADS * HEAD_DIM
    depth = norm_ffn1.shape[0]
    mp, ms = bp * lp, bs * ls
    tm_p = min(1024, mp)
    half = gm_w_in.shape[2] // 2

    xp = x_prompt.reshape(mp, d)
    xs = x_sample.reshape(ms, d)
    k_p, v_p, lf_p, k_s, v_s, lf_s, gv_s = [], [], [], [], [], [], []

    for i in range(depth):
        xp = _ffn(xp, norm_ffn1[i], ffn1_w_up, ffn1_w_down, i, tm_up=tm_p, tm_down=512)
        xs = _ffn(xs, norm_ffn1[i], ffn1_w_up, ffn1_w_down, i, tm_up=ms, tm_down=ms)
        j = i // 2
        if i % 2 == 0:
            (zp,) = _rms_proj(xp, norm_mix[i], gm_w_in, j, col0=0, ncols=2 * half, mode="gelu", tm=tm_p)
            (zs,) = _rms_proj(xs, norm_mix[i], gm_w_in, j, col0=0, ncols=2 * half, mode="gelu", tm=ms)
            gp = _gmlp_mix(zp, gm_ln_g[j], gm_ln_b[j], gm_w_s[j], gm_b_s[j].T)
            w00 = jnp.repeat(gm_w_s[j, :, 0, 0], GMLP_GROUP)
            b0 = jnp.repeat(gm_b_s[j, :, 0], GMLP_GROUP)
            gs, gv = _gmlp_mix_first_row(zs, gm_ln_g[j], gm_ln_b[j], w00, b0)
            gv_s.append(gv.reshape(bs, ls, half))
            xp = _mm_res(gp, gm_w_out, j, xp, 1.0, tm=512)
            xs = _mm_res(gs, gm_w_out, j, xs, 1.0, tm=ms)
        else:
            w_f = fox_w_in[j, :, 4 * d:]
            proj = functools.partial(_rms_proj, g=norm_mix[i], w=fox_w_in, layer=j, ncols=d)
            (qp,) = proj(xp, col0=0, mode="q", tm=tm_p, gain=fox_q_norm[j])
            kp, kp16 = proj(xp, col0=d, mode="k", tm=tm_p, gain=fox_k_norm[j])
            vp, vp16 = proj(xp, col0=2 * d, mode="v", tm=tm_p)
            (gatep,) = proj(xp, col0=3 * d, mode="gate", tm=tm_p)
            lfp = _logf_proj(xp, norm_mix[i], w_f, fox_b_f[j], tm=tm_p).reshape(bp, lp, N_HEADS)
            c = _cumsum_seq(lfp)
            ct = jnp.swapaxes(c, 1, 2).reshape(bp, N_HEADS, 1, lp)
            og = _flash_prompt(qp.reshape(bp, lp, d), kp16.reshape(bp, lp, d), vp16.reshape(bp, lp, d),
                               c, ct, gatep.reshape(bp, lp, d))
            xp = _mm_res(og.reshape(mp, d), fox_w_out, j, xp, 1.0, tm=512)
            k_p.append(kp.reshape(bp, lp, N_HEADS, HEAD_DIM))
            v_p.append(vp.reshape(bp, lp, N_HEADS, HEAD_DIM))
            lf_p.append(lfp)
            (qs,) = proj(xs, col0=0, mode="q", tm=ms, gain=fox_q_norm[j])
            ks, _ = proj(xs, col0=d, mode="k", tm=ms, gain=fox_k_norm[j])
            vs, _ = proj(xs, col0=2 * d, mode="v", tm=ms)
            (gates,) = proj(xs, col0=3 * d, mode="gate", tm=ms)
            lfs = _logf_proj(xs, norm_mix[i], w_f, fox_b_f[j], tm=ms)
            q_pad = jnp.pad(qs.reshape(bs, N_HEADS, HEAD_DIM), ((0, 0), (0, LANES - N_HEADS), (0, 0)))
            ks3 = ks.reshape(bs, N_HEADS, HEAD_DIM)
            vs3 = vs.reshape(bs, N_HEADS, HEAD_DIM)
            s_all = _dec_scores(page_table, q_pad, cache_k, cache_logf, j, ks3, lfs.reshape(bs, 1, N_HEADS))
            os_ = _dec_pv(page_table, s_all, cache_v, j, vs3, gates.reshape(bs, N_HEADS, HEAD_DIM))
            xs = _mm_res(os_.reshape(ms, d).astype(bf16), fox_w_out, j, xs, 1.0, tm=ms)
            k_s.append(ks.reshape(bs, ls, N_HEADS, HEAD_DIM))
            v_s.append(vs.reshape(bs, ls, N_HEADS, HEAD_DIM))
            lf_s.append(lfs.reshape(bs, ls, N_HEADS))
        xp = _ffn(xp, norm_ffn2[i], ffn2_w_up, ffn2_w_down, i, tm_up=tm_p, tm_down=512)
        xs = _ffn(xs, norm_ffn2[i], ffn2_w_up, ffn2_w_down, i, tm_up=ms, tm_down=ms)

    y_prompt = _final_norm(xp, norm_final, tm=512).reshape(bp, lp, d)
    y_sample = _final_norm(xs, norm_final, tm=ms).reshape(bs, ls, d)
    return (y_prompt, y_sample, jnp.stack(k_p), jnp.stack(v_p), jnp.stack(lf_p),
            jnp.stack(k_s), jnp.stack(v_s), jnp.stack(lf_s), jnp.stack(gv_s))
```

```python
import functools

import jax
import jax.numpy as jnp
from jax import lax
from jax.experimental import pallas as pl
from jax.experimental.pallas import tpu as pltpu

f32 = jnp.float32
bf16 = jnp.bfloat16

N_HEADS = 16
HEAD_DIM = 128
CHUNK = 128
GMLP_GROUP = 128
NORM_EPS = 1e-6
LN_EPS = 1e-5
FFN_RESIDUAL = 0.5
LANES = 128
SUBLANES = 8

VMEM_LIMIT_BYTES = 56 * 1024 * 1024
HIGHEST = lax.Precision.HIGHEST


def _params(*sem):
    return pltpu.CompilerParams(dimension_semantics=sem, vmem_limit_bytes=VMEM_LIMIT_BYTES)


def _rms_rows(x, g):
    ms = jnp.mean(x * x, axis=-1, keepdims=True)
    return x * lax.rsqrt(ms + NORM_EPS) * g


def _sigmoid(x):
    return 1.0 / (1.0 + jnp.exp(-x))


def _ffn_up_kernel(x_ref, g_ref, wa_ref, wb_ref, o_ref, xn_ref, *, nb, tn, nvalid, sub):
    j = pl.program_id(1)

    @pl.when(j == 0)
    def _():
        xn_ref[...] = _rms_rows(x_ref[...], g_ref[...]).astype(bf16)

    def emit(ncols, b_off):
        xn = xn_ref[...]
        for c0 in range(0, ncols, sub):
            c1 = min(c0 + sub, ncols)
            a = jnp.dot(xn, wa_ref[0, :, c0:c1].astype(bf16), preferred_element_type=f32)
            b = jnp.dot(xn, wb_ref[0, :, b_off + c0:b_off + c1].astype(bf16), preferred_element_type=f32)
            o_ref[:, c0:c1] = (a * _sigmoid(a) * b).astype(o_ref.dtype)

    if nvalid == tn:
        emit(tn, 0)
    else:
        @pl.when(j < nb - 1)
        def _():
            emit(tn, 0)

        @pl.when(j == nb - 1)
        def _():
            emit(nvalid, tn - nvalid)


def _ffn_up(x, g, w_up, layer, *, tm, tn=512, sub=256):
    m, d = x.shape
    dff = w_up.shape[2] // 2
    nb = pl.cdiv(dff, tn)
    nvalid = dff - (nb - 1) * tn
    kern = functools.partial(_ffn_up_kernel, nb=nb, tn=tn, nvalid=nvalid, sub=sub)
    return pl.pallas_call(
        kern,
        grid=(m // tm, nb),
        in_specs=[
            pl.BlockSpec((tm, d), lambda i, j: (i, 0)),
            pl.BlockSpec((1, d), lambda i, j: (0, 0)),
            pl.BlockSpec((1, d, tn), lambda i, j: (layer, 0, j)),
            pl.BlockSpec((pl.Element(1), pl.Element(d), pl.Element(tn)),
                         lambda i, j: (layer, 0, pl.multiple_of(jnp.minimum(dff + j * tn, 2 * dff - tn), LANES))),
        ],
        out_specs=pl.BlockSpec((tm, tn), lambda i, j: (i, j)),
        out_shape=jax.ShapeDtypeStruct((m, dff), bf16),
        scratch_shapes=[pltpu.VMEM((tm, d), bf16)],
        compiler_params=_params("parallel", "arbitrary"),
        name="ffn_up",
    )(x, g.reshape(1, d), w_up, w_up)


def _mm_res_kernel(h_ref, w_ref, r_ref, o_ref, wb_ref, *, scale):
    @pl.when(pl.program_id(1) == 0)
    def _():
        wb_ref[...] = w_ref[0].astype(bf16)

    acc = jnp.dot(h_ref[...], wb_ref[...], preferred_element_type=f32)
    o_ref[...] = r_ref[...] + scale * acc


def _mm_res(h, w, layer, res, scale, *, tm, tn=512):
    m, k = h.shape
    n = w.shape[2]
    return pl.pallas_call(
        functools.partial(_mm_res_kernel, scale=scale),
        grid=(n // tn, m // tm),
        in_specs=[
            pl.BlockSpec((tm, k), lambda j, i: (i, 0)),
            pl.BlockSpec((1, k, tn), lambda j, i: (layer, 0, j)),
            pl.BlockSpec((tm, tn), lambda j, i: (i, j)),
        ],
        out_specs=pl.BlockSpec((tm, tn), lambda j, i: (i, j)),
        out_shape=jax.ShapeDtypeStruct((m, n), f32),
        scratch_shapes=[pltpu.VMEM((k, tn), bf16)],
        compiler_params=_params("parallel", "arbitrary"),
        name="mm_res",
    )(h, w, res)


def _ffn(x, g, w_up, w_down, layer, *, tm_up, tm_down):
    h = _ffn_up(x, g, w_up, layer, tm=tm_up)
    return _mm_res(h, w_down, layer, x, FFN_RESIDUAL, tm=tm_down)


def _head_rms(acc, gain, tn):
    outs = []
    for hh in range(tn // HEAD_DIM):
        blk = acc[:, hh * HEAD_DIM:(hh + 1) * HEAD_DIM]
        ms = jnp.mean(blk * blk, axis=-1, keepdims=True)
        outs.append(blk * lax.rsqrt(ms + NORM_EPS) * gain)
    return jnp.concatenate(outs, axis=1)


def _rms_proj_kernel(*refs, mode, tn, n_out, trans_w):
    x_ref, g_ref, w_ref = refs[:3]
    extra = refs[3:-n_out - 1]
    outs = refs[-n_out - 1:-1]
    xn_ref = refs[-1]

    @pl.when(pl.program_id(1) == 0)
    def _():
        xn_ref[...] = _rms_rows(x_ref[...], g_ref[...]).astype(bf16)

    if trans_w:
        acc = lax.dot_general(xn_ref[...], w_ref[0].astype(bf16), (((1,), (1,)), ((), ())),
                              preferred_element_type=f32)
    else:
        acc = jnp.dot(xn_ref[...], w_ref[0].astype(bf16), preferred_element_type=f32)
    if mode == "gelu":
        outs[0][...] = 0.5 * acc * (1.0 + lax.erf(acc * (0.5 ** 0.5)))
    elif mode == "q":
        outs[0][...] = (_head_rms(acc, extra[0][...], tn) * (HEAD_DIM ** -0.5)).astype(bf16)
    elif mode == "k":
        kn = _head_rms(acc, extra[0][...], tn)
        outs[0][...] = kn
        outs[1][...] = kn.astype(bf16)
    elif mode == "v":
        outs[0][...] = acc
        outs[1][...] = acc.astype(bf16)
    elif mode == "gate":
        outs[0][...] = _sigmoid(acc)
    else:
        raise ValueError(mode)


def _rms_proj(x, g, w, layer, *, col0, ncols, mode, tm, tn=512, gain=None, trans_w=False, stack=None):
    m, d = x.shape
    out_dtypes = {"gelu": (f32,), "q": (bf16,), "k": (f32, bf16), "v": (f32, bf16), "gate": (f32,)}[mode]
    if trans_w:
        w_spec = pl.BlockSpec((1, tn, d), lambda i, j: (layer, col0 // tn + j, 0))
    else:
        w_spec = pl.BlockSpec((1, d, tn), lambda i, j: (layer, 0, col0 // tn + j))
    in_specs = [pl.BlockSpec((tm, d), lambda i, j: (i, 0)), pl.BlockSpec((1, d), lambda i, j: (0, 0)), w_spec]
    args = [x, g.reshape(1, d), w]
    if gain is not None:
        in_specs.append(pl.BlockSpec((1, HEAD_DIM), lambda i, j: (0, 0)))
        args.append(gain.reshape(1, HEAD_DIM))
    out_specs = [pl.BlockSpec((tm, tn), lambda i, j: (i, j)) for _ in out_dtypes]
    out_shape = [jax.ShapeDtypeStruct((m, ncols), dt) for dt in out_dtypes]
    aliases = {}
    if stack is not None:
        buf, slot, n_slots = stack
        out_specs[0] = pl.BlockSpec((None, tm, tn), lambda i, j: (slot, i, j))
        out_shape[0] = jax.ShapeDtypeStruct((n_slots, m, ncols), out_dtypes[0])
        if buf is not None:
            in_specs.append(pl.BlockSpec(memory_space=pl.ANY))
            args.append(buf)
            aliases = {len(args) - 1: 0}
    out = pl.pallas_call(
        functools.partial(_rms_proj_kernel, mode=mode, tn=tn, n_out=len(out_dtypes), trans_w=trans_w),
        grid=(m // tm, ncols // tn),
        in_specs=in_specs,
        out_specs=out_specs,
        out_shape=out_shape,
        input_output_aliases=aliases,
        scratch_shapes=[pltpu.VMEM((tm, d), bf16)],
        compiler_params=_params("parallel", "arbitrary"),
        name="rms_proj_" + mode,
    )(*args)
    return out


def _logf_kernel(x_ref, g_ref, w_ref, b_ref, o_ref):
    xn = _rms_rows(x_ref[...], g_ref[...]).astype(bf16)
    z = lax.dot_general(xn, w_ref[...].astype(bf16), (((1,), (1,)), ((), ())),
                        preferred_element_type=f32) + b_ref[...]
    o_ref[...] = jnp.minimum(z, 0.0) - jnp.log1p(jnp.exp(-jnp.abs(z)))


def _logf_proj(x, g, w_ft, b_f, *, tm):
    m, d = x.shape
    return pl.pallas_call(
        _logf_kernel,
        grid=(m // tm,),
        in_specs=[
            pl.BlockSpec((tm, d), lambda i: (i, 0)),
            pl.BlockSpec((1, d), lambda i: (0, 0)),
            pl.BlockSpec((N_HEADS, d), lambda i: (0, 0)),
            pl.BlockSpec((1, N_HEADS), lambda i: (0, 0)),
        ],
        out_specs=pl.BlockSpec((tm, N_HEADS), lambda i: (i, 0)),
        out_shape=jax.ShapeDtypeStruct((m, N_HEADS), f32),
        compiler_params=_params("parallel"),
        name="logf_proj",
    )(x, g.reshape(1, d), w_ft, b_f.reshape(1, N_HEADS))


def _layer_norm_rows(v, g, b):
    mu = jnp.mean(v, axis=-1, keepdims=True)
    vc = v - mu
    var = jnp.mean(vc * vc, axis=-1, keepdims=True)
    return vc * lax.rsqrt(var + LN_EPS) * g + b


def _gmlp_mix_kernel(u_ref, v_ref, g_ref, b_ref, ws_ref, bs_ref, o_ref, *, n_groups, n_chunks):
    vn = _layer_norm_rows(v_ref[...], g_ref[...], b_ref[...]).astype(bf16)
    row = lax.broadcasted_iota(jnp.int32, (CHUNK, CHUNK), 0)
    col = lax.broadcasted_iota(jnp.int32, (CHUNK, CHUNK), 1)
    causal = row >= col
    for gi in range(n_groups):
        cols = slice(gi * GMLP_GROUP, (gi + 1) * GMLP_GROUP)
        ws = jnp.where(causal, ws_ref[gi], 0.0).astype(bf16)
        rhs = jnp.concatenate([vn[c * CHUNK:(c + 1) * CHUNK, cols] for c in range(n_chunks)], axis=1)
        mixed = jnp.dot(ws, rhs, preferred_element_type=f32) + bs_ref[:, gi:gi + 1]
        for c in range(n_chunks):
            rows = slice(c * CHUNK, (c + 1) * CHUNK)
            o_ref[rows, cols] = (u_ref[rows, cols] * mixed[:, c * GMLP_GROUP:(c + 1) * GMLP_GROUP]).astype(bf16)


def _gmlp_mix(z, ln_g, ln_b, w_s, b_s_t, *, n_chunks=2):
    m = z.shape[0]
    half = z.shape[1] // 2
    n_groups = half // GMLP_GROUP
    tm = n_chunks * CHUNK
    return pl.pallas_call(
        functools.partial(_gmlp_mix_kernel, n_groups=n_groups, n_chunks=n_chunks),
        grid=(m // tm,),
        in_specs=[
            pl.BlockSpec((tm, half), lambda i: (i, 0)),
            pl.BlockSpec((tm, half), lambda i: (i, 1)),
            pl.BlockSpec((1, half), lambda i: (0, 0)),
            pl.BlockSpec((1, half), lambda i: (0, 0)),
            pl.BlockSpec((n_groups, CHUNK, CHUNK), lambda i: (0, 0, 0)),
            pl.BlockSpec((CHUNK, n_groups), lambda i: (0, 0)),
        ],
        out_specs=pl.BlockSpec((tm, half), lambda i: (i, 0)),
        out_shape=jax.ShapeDtypeStruct((m, half), bf16),
        compiler_params=_params("parallel"),
        name="gmlp_mix",
    )(z, z, ln_g.reshape(1, half), ln_b.reshape(1, half), w_s, b_s_t)


def _gmlp_mix_first_row_kernel(u_ref, v_ref, g_ref, b_ref, w0_ref, b0_ref, o_ref, vn_ref):
    vn = _layer_norm_rows(v_ref[...], g_ref[...], b_ref[...])
    vn_ref[...] = vn
    mixed = vn * w0_ref[...] + b0_ref[...]
    o_ref[...] = (u_ref[...] * mixed).astype(bf16)


def _gmlp_mix_first_row(z, ln_g, ln_b, w00, b0):
    m = z.shape[0]
    half = z.shape[1] // 2
    row = lambda: pl.BlockSpec((1, half), lambda i: (0, 0))
    return pl.pallas_call(
        _gmlp_mix_first_row_kernel,
        grid=(1,),
        in_specs=[
            pl.BlockSpec((m, half), lambda i: (0, 0)),
            pl.BlockSpec((m, half), lambda i: (0, 1)),
            row(), row(), row(), row(),
        ],
        out_specs=[pl.BlockSpec((m, half), lambda i: (0, 0)), pl.BlockSpec((m, half), lambda i: (0, 0))],
        out_shape=[jax.ShapeDtypeStruct((m, half), bf16), jax.ShapeDtypeStruct((m, half), f32)],
        compiler_params=_params("arbitrary"),
        name="gmlp_mix_first_row",
    )(z, z, ln_g.reshape(1, half), ln_b.reshape(1, half), w00.reshape(1, half), b0.reshape(1, half))


def _cumsum_kernel(x_ref, o_ref, *, n_blocks):
    row = lax.broadcasted_iota(jnp.int32, (CHUNK, CHUNK), 0)
    col = lax.broadcasted_iota(jnp.int32, (CHUNK, CHUNK), 1)
    tri = (row >= col).astype(f32)
    carry = jnp.zeros((1, N_HEADS), f32)
    for blk in range(n_blocks):
        rows = slice(blk * CHUNK, (blk + 1) * CHUNK)
        cs = jnp.dot(tri, x_ref[0, rows, :], precision=HIGHEST, preferred_element_type=f32) + carry
        o_ref[0, rows, :] = cs
        carry = cs[CHUNK - 1:CHUNK, :]


def _cumsum_seq(lf):
    b, l, h = lf.shape
    return pl.pallas_call(
        functools.partial(_cumsum_kernel, n_blocks=l // CHUNK),
        grid=(b,),
        in_specs=[pl.BlockSpec((1, l, h), lambda i: (i, 0, 0))],
        out_specs=pl.BlockSpec((1, l, h), lambda i: (i, 0, 0)),
        out_shape=jax.ShapeDtypeStruct((b, l, h), f32),
        compiler_params=_params("parallel"),
        name="cumsum_seq",
    )(lf)


def _flash_kernel(q_ref, k_ref, v_ref, c_ref, ct_ref, gate_ref, o_ref, *, tq):
    h = pl.program_id(1)
    qi = pl.program_id(2)
    q = q_ref[0]
    lane = lax.broadcasted_iota(jnp.int32, (tq, N_HEADS), 1)
    cq = jnp.sum(jnp.where(lane == h, c_ref[0], 0.0), axis=1, keepdims=True)

    def scores(kj):
        start = pl.multiple_of(kj * tq, tq)
        k = k_ref[0, pl.ds(start, tq), :]
        s = lax.dot_general(q, k, (((1,), (1,)), ((), ())), preferred_element_type=f32)
        return s + (cq - ct_ref[0, 0, :, pl.ds(start, tq)])

    def update(kj, s, carry):
        m, l, acc = carry
        start = pl.multiple_of(kj * tq, tq)
        m_new = jnp.maximum(m, jnp.max(s, axis=1, keepdims=True))
        p = jnp.exp(s - m_new)
        alpha = jnp.exp(m - m_new)
        l = alpha * l + jnp.sum(p, axis=1, keepdims=True)
        pv = jnp.dot(p.astype(bf16), v_ref[0, pl.ds(start, tq), :], preferred_element_type=f32)
        return m_new, l, alpha * acc + pv

    def body(kj, carry):
        return update(kj, scores(kj), carry)

    init = (jnp.full((tq, 1), -jnp.inf, f32), jnp.zeros((tq, 1), f32), jnp.zeros((tq, HEAD_DIM), f32))
    carry = lax.fori_loop(0, qi, body, init)
    row = lax.broadcasted_iota(jnp.int32, (tq, tq), 0)
    col = lax.broadcasted_iota(jnp.int32, (tq, tq), 1)
    m, l, acc = update(qi, jnp.where(col <= row, scores(qi), -jnp.inf), carry)
    o_ref[0] = ((acc / l) * gate_ref[0]).astype(bf16)


def _flash_prompt(q, k, v, c, ct, gate, *, tq=512):
    b, l, d = q.shape
    blk = lambda: pl.BlockSpec((1, tq, HEAD_DIM), lambda bi, h, qi: (bi, qi, h))
    seq = lambda: pl.BlockSpec((1, l, HEAD_DIM), lambda bi, h, qi: (bi, 0, h))
    return pl.pallas_call(
        functools.partial(_flash_kernel, tq=tq),
        grid=(b, N_HEADS, l // tq),
        in_specs=[
            blk(), seq(), seq(),
            pl.BlockSpec((1, tq, N_HEADS), lambda bi, h, qi: (bi, qi, 0)),
            pl.BlockSpec((1, 1, 1, l), lambda bi, h, qi: (bi, h, 0, 0)),
            blk(),
        ],
        out_specs=blk(),
        out_shape=jax.ShapeDtypeStruct((b, l, d), bf16),
        compiler_params=_params("parallel", "parallel", "arbitrary"),
        name="flash_prompt",
    )(q, k, v, c, ct, gate)


T_PER_ROW = LANES // N_HEADS
PAGES_PER_STEP = 4


def _eye(n, m):
    return lax.broadcasted_iota(jnp.int32, (n, m), 0) == lax.broadcasted_iota(jnp.int32, (n, m), 1)


def _lane_allreduce(x, op):
    sh = N_HEADS
    while sh < LANES:
        x = op(x, pltpu.roll(x, sh, 1))
        sh *= 2
    return x


def _flat_suffix(x):
    rows = x.shape[0]
    lane = lax.broadcasted_iota(jnp.int32, x.shape, 1)
    row = lax.broadcasted_iota(jnp.int32, x.shape, 0)
    y = x
    sh = N_HEADS
    while sh < LANES:
        y = y + jnp.where(lane < LANES - sh, pltpu.roll(y, LANES - sh, 1), 0.0)
        sh *= 2
    z = _lane_allreduce(jnp.where(lane < N_HEADS, y, 0.0), jnp.add)
    w = z
    sh = 1
    while sh < rows:
        w = w + jnp.where(row < rows - sh, pltpu.roll(w, rows - sh, 0), 0.0)
        sh *= 2
    return (y - x) + (w - z), w[0:1, :]


def _flat_scores(k3, q_rep):
    t = k3.shape[0]
    k2 = k3.reshape(t * N_HEADS, HEAD_DIM).astype(bf16)
    s = lax.dot_general(k2, q_rep, (((1,), (1,)), ((), ())), preferred_element_type=f32)
    s3 = s.reshape(t * N_HEADS // LANES, LANES, LANES)
    return jnp.sum(jnp.where(_eye(LANES, LANES)[None], s3, 0.0), axis=1)


def _dec_scores_kernel(pt_ref, q_ref, *refs, n_steps):
    g_pages = PAGES_PER_STEP
    k_refs, lf_refs = refs[:g_pages], refs[g_pages:2 * g_pages]
    knew_ref, lfnew_ref, s_ref, snew_ref, carry_ref = refs[2 * g_pages:]
    p = pl.program_id(1)

    @pl.when(p == 0)
    def _():
        carry_ref[...] = jnp.zeros_like(carry_ref)

    q_rep = q_ref[0]
    carry = carry_ref[...]
    for g in range(g_pages):
        later, total = _flat_suffix(lf_refs[g][0, 0])
        s_ref[0, g_pages - 1 - g] = _flat_scores(k_refs[g][0, 0], q_rep) + (later + carry + lfnew_ref[0])
        carry = carry + total
    carry_ref[...] = carry

    @pl.when(p == n_steps - 1)
    def _():
        s_new = lax.dot_general(knew_ref[0].astype(bf16), q_rep, (((1,), (1,)), ((), ())),
                                preferred_element_type=f32)
        s_new = jnp.sum(jnp.where(_eye(N_HEADS, LANES), s_new, 0.0), axis=0, keepdims=True)
        row = lax.broadcasted_iota(jnp.int32, snew_ref.shape[1:], 0)
        lane = lax.broadcasted_iota(jnp.int32, snew_ref.shape[1:], 1)
        snew_ref[0] = jnp.where((row == 0) & (lane < N_HEADS), s_new, -jnp.inf)


def _dec_scores(page_table, q_rep, cache_k, lf_flat, layer, k_new, lf_new_rep):
    nb, n_pages = page_table.shape
    page = cache_k.shape[2]
    g_pages = PAGES_PER_STEP
    n_steps = n_pages // g_pages
    rows = page // T_PER_ROW

    def past(g):
        return lambda b, p, pt: pt[b, n_pages - 1 - (p * g_pages + g)]

    k_specs = [pl.BlockSpec((1, 1, page, N_HEADS, HEAD_DIM),
                            lambda b, p, pt, f=past(g): (layer, f(b, p, pt), 0, 0, 0)) for g in range(g_pages)]
    lf_specs = [pl.BlockSpec((1, 1, rows, LANES),
                             lambda b, p, pt, f=past(g): (layer, f(b, p, pt), 0, 0)) for g in range(g_pages)]
    return pl.pallas_call(
        functools.partial(_dec_scores_kernel, n_steps=n_steps),
        grid_spec=pltpu.PrefetchScalarGridSpec(
            num_scalar_prefetch=1,
            grid=(nb, n_steps),
            in_specs=[pl.BlockSpec((1, LANES, HEAD_DIM), lambda b, p, pt: (b, 0, 0))] + k_specs + lf_specs + [
                pl.BlockSpec((1, N_HEADS, HEAD_DIM), lambda b, p, pt: (b, 0, 0)),
                pl.BlockSpec((1, 1, LANES), lambda b, p, pt: (b, 0, 0)),
            ],
            out_specs=[
                pl.BlockSpec((1, g_pages, rows, LANES), lambda b, p, pt: (b, n_steps - 1 - p, 0, 0)),
                pl.BlockSpec((1, SUBLANES, LANES), lambda b, p, pt: (b, 0, 0)),
            ],
            scratch_shapes=[pltpu.VMEM((1, LANES), f32)],
        ),
        out_shape=[jax.ShapeDtypeStruct((nb, n_pages, rows, LANES), f32),
                   jax.ShapeDtypeStruct((nb, SUBLANES, LANES), f32)],
        compiler_params=_params("parallel", "arbitrary"),
        name="dec_scores",
    )(page_table, q_rep, *([cache_k] * g_pages), *([lf_flat] * g_pages), k_new, lf_new_rep)


def _lane_broadcast_rows(flat):
    r = flat.shape[0]
    spread = jnp.where(_eye(LANES, LANES)[None], jnp.broadcast_to(flat[:, None, :], (r, LANES, LANES)), 0.0)
    ones = jnp.ones((LANES, HEAD_DIM), bf16)
    return jnp.dot(spread.reshape(r * LANES, LANES).astype(bf16), ones, preferred_element_type=f32)


def _dec_pv_kernel(pt_ref, s_ref, snew_ref, *refs, n_steps):
    g_pages = PAGES_PER_STEP
    v_refs = refs[:g_pages]
    vnew_ref, gate_ref, o_ref, p_ref, pnew_ref, acc_ref = refs[g_pages:]
    p = pl.program_id(1)

    @pl.when(p == 0)
    def _():
        s = s_ref[0]
        s_new = snew_ref[0]
        m = jnp.maximum(jnp.max(jnp.max(s, axis=0), axis=0, keepdims=True), jnp.max(s_new, axis=0, keepdims=True))
        m = _lane_allreduce(m, jnp.maximum)
        e = jnp.exp(s - m[None])
        e_new = jnp.exp(s_new - m)
        denom = jnp.sum(jnp.sum(e, axis=0), axis=0, keepdims=True) + jnp.sum(e_new, axis=0, keepdims=True)
        denom = _lane_allreduce(denom, jnp.add)
        p_ref[...] = e / denom[None]
        pnew_ref[...] = e_new / denom
        acc_ref[...] = jnp.zeros_like(acc_ref)

    acc = acc_ref[...]
    for g in range(g_pages):
        v3 = v_refs[g][0, 0]
        pb = _lane_broadcast_rows(p_ref[p * g_pages + g])
        acc = acc + jnp.sum(pb.reshape(v3.shape) * v3, axis=0)
    acc_ref[...] = acc

    @pl.when(p == n_steps - 1)
    def _():
        p_new = jnp.where(_eye(N_HEADS, LANES), jnp.broadcast_to(pnew_ref[0:1, :], (N_HEADS, LANES)), 0.0)
        pb = jnp.dot(p_new.astype(bf16), jnp.ones((LANES, HEAD_DIM), bf16), preferred_element_type=f32)
        o_ref[0] = (acc + pb * vnew_ref[0]) * gate_ref[0]


def _dec_pv(page_table, s_all, s_new, cache_v, layer, v_new, gate):
    nb, n_pages = page_table.shape
    page = cache_v.shape[2]
    g_pages = PAGES_PER_STEP
    n_steps = n_pages // g_pages
    rows = page // T_PER_ROW
    head_blk = lambda: pl.BlockSpec((1, N_HEADS, HEAD_DIM), lambda b, p, pt: (b, 0, 0))
    v_specs = [pl.BlockSpec((1, 1, page, N_HEADS, HEAD_DIM),
                            lambda b, p, pt, g=g: (layer, pt[b, p * g_pages + g], 0, 0, 0)) for g in range(g_pages)]
    return pl.pallas_call(
        functools.partial(_dec_pv_kernel, n_steps=n_steps),
        grid_spec=pltpu.PrefetchScalarGridSpec(
            num_scalar_prefetch=1,
            grid=(nb, n_steps),
            in_specs=[
                pl.BlockSpec((1, n_pages, rows, LANES), lambda b, p, pt: (b, 0, 0, 0)),
                pl.BlockSpec((1, SUBLANES, LANES), lambda b, p, pt: (b, 0, 0)),
            ] + v_specs + [head_blk(), head_blk()],
            out_specs=head_blk(),
            scratch_shapes=[pltpu.VMEM((n_pages, rows, LANES), f32), pltpu.VMEM((SUBLANES, LANES), f32),
                            pltpu.VMEM((N_HEADS, HEAD_DIM), f32)],
        ),
        out_shape=jax.ShapeDtypeStruct((nb, N_HEADS, HEAD_DIM), f32),
        compiler_params=_params("parallel", "arbitrary"),
        name="dec_pv",
    )(page_table, s_all, s_new, *([cache_v] * g_pages), v_new, gate)


def _final_norm_kernel(x_ref, g_ref, o_ref):
    o_ref[...] = _rms_rows(x_ref[...], g_ref[...])


def _final_norm(x, g, *, tm):
    m, d = x.shape
    return pl.pallas_call(
        _final_norm_kernel,
        grid=(m // tm,),
        in_specs=[pl.BlockSpec((tm, d), lambda i: (i, 0)), pl.BlockSpec((1, d), lambda i: (0, 0))],
        out_specs=pl.BlockSpec((tm, d), lambda i: (i, 0)),
        out_shape=jax.ShapeDtypeStruct((m, d), f32),
        compiler_params=_params("parallel"),
        name="final_norm",
    )(x, g.reshape(1, d))


def kernel(x_prompt, x_sample, cache_k, cache_v, cache_logf, page_table, norm_ffn1, ffn1_w_up, ffn1_w_down, norm_mix, norm_ffn2, ffn2_w_up, ffn2_w_down, gm_w_in, gm_ln_g, gm_ln_b, gm_w_s, gm_b_s, gm_w_out, fox_w_in, fox_b_f, fox_q_norm, fox_k_norm, fox_w_out, norm_final):
    bp, lp, d = x_prompt.shape
    bs, ls, _ = x_sample.shape
    assert ls == 1 and lp % 512 == 0 and d == N_HEADS * HEAD_DIM
    depth = norm_ffn1.shape[0]
    mp, ms = bp * lp, bs * ls
    tm_p = min(1024, mp)
    half = gm_w_in.shape[2] // 2

    xp = x_prompt.reshape(mp, d)
    xs = x_sample.reshape(ms, d)
    lf_p, lf_s, gv_s = [], [], []
    kp_all = vp_all = ks_all = vs_all = None
    n_fox = fox_w_in.shape[0]
    n_pool, page = cache_logf.shape[1:3]
    assert page == LANES and page_table.shape[1] % PAGES_PER_STEP == 0
    fox_w_t = jnp.swapaxes(fox_w_in, 1, 2)
    lf_flat = cache_logf.reshape(n_fox, n_pool, page // T_PER_ROW, LANES)

    for i in range(depth):
        xp = _ffn(xp, norm_ffn1[i], ffn1_w_up, ffn1_w_down, i, tm_up=tm_p, tm_down=512)
        xs = _ffn(xs, norm_ffn1[i], ffn1_w_up, ffn1_w_down, i, tm_up=ms, tm_down=ms)
        j = i // 2
        if i % 2 == 0:
            (zp,) = _rms_proj(xp, norm_mix[i], gm_w_in, j, col0=0, ncols=2 * half, mode="gelu", tm=tm_p)
            (zs,) = _rms_proj(xs, norm_mix[i], gm_w_in, j, col0=0, ncols=2 * half, mode="gelu", tm=ms)
            gp = _gmlp_mix(zp, gm_ln_g[j], gm_ln_b[j], gm_w_s[j], gm_b_s[j].T)
            w00 = jnp.repeat(gm_w_s[j, :, 0, 0], GMLP_GROUP)
            b0 = jnp.repeat(gm_b_s[j, :, 0], GMLP_GROUP)
            gs, gv = _gmlp_mix_first_row(zs, gm_ln_g[j], gm_ln_b[j], w00, b0)
            gv_s.append(gv.reshape(bs, ls, half))
            xp = _mm_res(gp, gm_w_out, j, xp, 1.0, tm=512)
            xs = _mm_res(gs, gm_w_out, j, xs, 1.0, tm=ms)
        else:
            w_ft = fox_w_t[j, 4 * d:, :]
            proj = functools.partial(_rms_proj, g=norm_mix[i], w=fox_w_t, layer=j, ncols=d, trans_w=True)
            (qp,) = proj(xp, col0=0, mode="q", tm=tm_p, gain=fox_q_norm[j])
            kp_all, kp16 = proj(xp, col0=d, mode="k", tm=tm_p, gain=fox_k_norm[j], stack=(kp_all, j, n_fox))
            vp_all, vp16 = proj(xp, col0=2 * d, mode="v", tm=tm_p, stack=(vp_all, j, n_fox))
            (gatep,) = proj(xp, col0=3 * d, mode="gate", tm=tm_p)
            lfp = _logf_proj(xp, norm_mix[i], w_ft, fox_b_f[j], tm=tm_p).reshape(bp, lp, N_HEADS)
            c = _cumsum_seq(lfp)
            ct = jnp.swapaxes(c, 1, 2).reshape(bp, N_HEADS, 1, lp)
            og = _flash_prompt(qp.reshape(bp, lp, d), kp16.reshape(bp, lp, d), vp16.reshape(bp, lp, d),
                               c, ct, gatep.reshape(bp, lp, d))
            xp = _mm_res(og.reshape(mp, d), fox_w_out, j, xp, 1.0, tm=512)
            lf_p.append(lfp)
            (qs,) = proj(xs, col0=0, mode="q", tm=ms, gain=fox_q_norm[j])
            ks_all, _ = proj(xs, col0=d, mode="k", tm=ms, gain=fox_k_norm[j], stack=(ks_all, j, n_fox))
            vs_all, _ = proj(xs, col0=2 * d, mode="v", tm=ms, stack=(vs_all, j, n_fox))
            (gates,) = proj(xs, col0=3 * d, mode="gate", tm=ms)
            lfs = _logf_proj(xs, norm_mix[i], w_ft, fox_b_f[j], tm=ms)
            q_rep = jnp.tile(qs.reshape(bs, N_HEADS, HEAD_DIM), (1, T_PER_ROW, 1))
            lf_new_rep = jnp.tile(lfs, (1, T_PER_ROW)).reshape(bs, 1, LANES)
            ks3 = ks_all[j].reshape(bs, N_HEADS, HEAD_DIM)
            vs3 = vs_all[j].reshape(bs, N_HEADS, HEAD_DIM)
            s_all, s_new = _dec_scores(page_table, q_rep, cache_k, lf_flat, j, ks3, lf_new_rep)
            os_ = _dec_pv(page_table, s_all, s_new, cache_v, j, vs3, gates.reshape(bs, N_HEADS, HEAD_DIM))
            xs = _mm_res(os_.reshape(ms, d).astype(bf16), fox_w_out, j, xs, 1.0, tm=ms)
            lf_s.append(lfs.reshape(bs, ls, N_HEADS))
        xp = _ffn(xp, norm_ffn2[i], ffn2_w_up, ffn2_w_down, i, tm_up=tm_p, tm_down=512)
        xs = _ffn(xs, norm_ffn2[i], ffn2_w_up, ffn2_w_down, i, tm_up=ms, tm_down=ms)

    y_prompt = _final_norm(xp, norm_final, tm=512).reshape(bp, lp, d)
    y_sample = _final_norm(xs, norm_final, tm=ms).reshape(bs, ls, d)
    heads_p = (n_fox, bp, lp, N_HEADS, HEAD_DIM)
    heads_s = (n_fox, bs, ls, N_HEADS, HEAD_DIM)
    return (y_prompt, y_sample, kp_all.reshape(heads_p), vp_all.reshape(heads_p), jnp.stack(lf_p),
            ks_all.reshape(heads_s), vs_all.reshape(heads_s), jnp.stack(lf_s), jnp.stack(gv_s))
```

```python
import functools

import jax
import jax.numpy as jnp
from jax import lax
from jax.experimental import pallas as pl
from jax.experimental.pallas import tpu as pltpu

f32 = jnp.float32
bf16 = jnp.bfloat16

N_HEADS = 16
HEAD_DIM = 128
CHUNK = 128
GMLP_GROUP = 128
NORM_EPS = 1e-6
LN_EPS = 1e-5
FFN_RESIDUAL = 0.5
LANES = 128
SUBLANES = 8

VMEM_LIMIT_BYTES = 56 * 1024 * 1024
HIGHEST = lax.Precision.HIGHEST
LOG2E = 1.4426950408889634
SCORE_SCALE = HEAD_DIM ** -0.5


def _params(*sem):
    return pltpu.CompilerParams(dimension_semantics=sem, vmem_limit_bytes=VMEM_LIMIT_BYTES)


def _rms_rows(x, g):
    ms = jnp.mean(x * x, axis=-1, keepdims=True)
    return x * lax.rsqrt(ms + NORM_EPS) * g


def _sigmoid(x):
    return 1.0 / (1.0 + jnp.exp(-x))


def _ffn_up_kernel(x_ref, g_ref, wa_ref, wb_ref, o_ref, xn_ref, *, nb, tn, nvalid, sub):
    j = pl.program_id(1)

    @pl.when(j == 0)
    def _():
        xn_ref[...] = _rms_rows(x_ref[...], g_ref[...]).astype(bf16)

    def emit(ncols, b_off):
        xn = xn_ref[...]
        for c0 in range(0, ncols, sub):
            c1 = min(c0 + sub, ncols)
            a = jnp.dot(xn, wa_ref[0, :, c0:c1].astype(bf16), preferred_element_type=f32)
            b = jnp.dot(xn, wb_ref[0, :, b_off + c0:b_off + c1].astype(bf16), preferred_element_type=f32)
            o_ref[:, c0:c1] = (a * _sigmoid(a) * b).astype(o_ref.dtype)

    if nvalid == tn:
        emit(tn, 0)
    else:
        @pl.when(j < nb - 1)
        def _():
            emit(tn, 0)

        @pl.when(j == nb - 1)
        def _():
            emit(nvalid, tn - nvalid)


def _ffn_up(x, g, w_up, layer, *, tm, tn=512, sub=256):
    m, d = x.shape
    dff = w_up.shape[2] // 2
    nb = pl.cdiv(dff, tn)
    nvalid = dff - (nb - 1) * tn
    kern = functools.partial(_ffn_up_kernel, nb=nb, tn=tn, nvalid=nvalid, sub=sub)
    return pl.pallas_call(
        kern,
        grid=(m // tm, nb),
        in_specs=[
            pl.BlockSpec((tm, d), lambda i, j: (i, 0)),
            pl.BlockSpec((1, d), lambda i, j: (0, 0)),
            pl.BlockSpec((1, d, tn), lambda i, j: (layer, 0, j)),
            pl.BlockSpec((pl.Element(1), pl.Element(d), pl.Element(tn)),
                         lambda i, j: (layer, 0, pl.multiple_of(jnp.minimum(dff + j * tn, 2 * dff - tn), LANES))),
        ],
        out_specs=pl.BlockSpec((tm, tn), lambda i, j: (i, j)),
        out_shape=jax.ShapeDtypeStruct((m, dff), bf16),
        scratch_shapes=[pltpu.VMEM((tm, d), bf16)],
        compiler_params=_params("parallel", "arbitrary"),
        name="ffn_up",
    )(x, g.reshape(1, d), w_up, w_up)


def _mm_res_kernel(h_ref, w_ref, r_ref, o_ref, wb_ref, *, scale):
    @pl.when(pl.program_id(1) == 0)
    def _():
        wb_ref[...] = w_ref[0].astype(bf16)

    acc = jnp.dot(h_ref[...], wb_ref[...], preferred_element_type=f32)
    o_ref[...] = r_ref[...] + scale * acc


def _mm_res(h, w, layer, res, scale, *, tm, tn=512):
    m, k = h.shape
    n = w.shape[2]
    return pl.pallas_call(
        functools.partial(_mm_res_kernel, scale=scale),
        grid=(n // tn, m // tm),
        in_specs=[
            pl.BlockSpec((tm, k), lambda j, i: (i, 0)),
            pl.BlockSpec((1, k, tn), lambda j, i: (layer, 0, j)),
            pl.BlockSpec((tm, tn), lambda j, i: (i, j)),
        ],
        out_specs=pl.BlockSpec((tm, tn), lambda j, i: (i, j)),
        out_shape=jax.ShapeDtypeStruct((m, n), f32),
        scratch_shapes=[pltpu.VMEM((k, tn), bf16)],
        compiler_params=_params("parallel", "arbitrary"),
        name="mm_res",
    )(h, w, res)


def _ffn(x, g, w_up, w_down, layer, *, tm_up, tm_down):
    h = _ffn_up(x, g, w_up, layer, tm=tm_up)
    return _mm_res(h, w_down, layer, x, FFN_RESIDUAL, tm=tm_down)


def _head_rms(acc, gain, tn):
    outs = []
    for hh in range(tn // HEAD_DIM):
        blk = acc[:, hh * HEAD_DIM:(hh + 1) * HEAD_DIM]
        ms = jnp.mean(blk * blk, axis=-1, keepdims=True)
        outs.append(blk * lax.rsqrt(ms + NORM_EPS) * gain)
    return jnp.concatenate(outs, axis=1)


def _rms_proj_kernel(*refs, mode, tn, n_out, trans_w, q_scale):
    x_ref, g_ref, w_ref = refs[:3]
    extra = refs[3:-n_out - 1]
    outs = refs[-n_out - 1:-1]
    xn_ref = refs[-1]

    @pl.when(pl.program_id(1) == 0)
    def _():
        xn_ref[...] = _rms_rows(x_ref[...], g_ref[...]).astype(bf16)

    if trans_w:
        acc = lax.dot_general(xn_ref[...], w_ref[0].astype(bf16), (((1,), (1,)), ((), ())),
                              preferred_element_type=f32)
    else:
        acc = jnp.dot(xn_ref[...], w_ref[0].astype(bf16), preferred_element_type=f32)
    if mode == "gelu":
        outs[0][...] = 0.5 * acc * (1.0 + lax.erf(acc * (0.5 ** 0.5)))
    elif mode == "q":
        outs[0][...] = (_head_rms(acc, extra[0][...], tn) * q_scale).astype(bf16)
    elif mode == "k":
        kn = _head_rms(acc, extra[0][...], tn)
        outs[0][...] = kn
        outs[1][...] = kn.astype(bf16)
    elif mode == "v":
        outs[0][...] = acc
        outs[1][...] = acc.astype(bf16)
    elif mode == "gate":
        outs[0][...] = _sigmoid(acc)
    else:
        raise ValueError(mode)


def _rms_proj(x, g, w, layer, *, col0, ncols, mode, tm, tn=512, gain=None, trans_w=False, stack=None,
              q_scale=None):
    m, d = x.shape
    out_dtypes = {"gelu": (f32,), "q": (bf16,), "k": (f32, bf16), "v": (f32, bf16), "gate": (f32,)}[mode]
    if trans_w:
        w_spec = pl.BlockSpec((1, tn, d), lambda i, j: (layer, col0 // tn + j, 0))
    else:
        w_spec = pl.BlockSpec((1, d, tn), lambda i, j: (layer, 0, col0 // tn + j))
    in_specs = [pl.BlockSpec((tm, d), lambda i, j: (i, 0)), pl.BlockSpec((1, d), lambda i, j: (0, 0)), w_spec]
    args = [x, g.reshape(1, d), w]
    if gain is not None:
        in_specs.append(pl.BlockSpec((1, HEAD_DIM), lambda i, j: (0, 0)))
        args.append(gain.reshape(1, HEAD_DIM))
    out_specs = [pl.BlockSpec((tm, tn), lambda i, j: (i, j)) for _ in out_dtypes]
    out_shape = [jax.ShapeDtypeStruct((m, ncols), dt) for dt in out_dtypes]
    aliases = {}
    if stack is not None:
        buf, slot, n_slots = stack
        out_specs[0] = pl.BlockSpec((None, tm, tn), lambda i, j: (slot, i, j))
        out_shape[0] = jax.ShapeDtypeStruct((n_slots, m, ncols), out_dtypes[0])
        if buf is not None:
            in_specs.append(pl.BlockSpec(memory_space=pl.ANY))
            args.append(buf)
            aliases = {len(args) - 1: 0}
    out = pl.pallas_call(
        functools.partial(_rms_proj_kernel, mode=mode, tn=tn, n_out=len(out_dtypes), trans_w=trans_w,
                          q_scale=q_scale),
        grid=(m // tm, ncols // tn),
        in_specs=in_specs,
        out_specs=out_specs,
        out_shape=out_shape,
        input_output_aliases=aliases,
        scratch_shapes=[pltpu.VMEM((tm, d), bf16)],
        compiler_params=_params("parallel", "arbitrary"),
        name="rms_proj_" + mode,
    )(*args)
    return out


def _logf_kernel(x_ref, g_ref, w_ref, b_ref, o_ref):
    xn = _rms_rows(x_ref[...], g_ref[...]).astype(bf16)
    z = lax.dot_general(xn, w_ref[...].astype(bf16), (((1,), (1,)), ((), ())),
                        preferred_element_type=f32) + b_ref[...]
    o_ref[...] = jnp.minimum(z, 0.0) - jnp.log1p(jnp.exp(-jnp.abs(z)))


def _logf_proj(x, g, w_ft, b_f, *, tm):
    m, d = x.shape
    return pl.pallas_call(
        _logf_kernel,
        grid=(m // tm,),
        in_specs=[
            pl.BlockSpec((tm, d), lambda i: (i, 0)),
            pl.BlockSpec((1, d), lambda i: (0, 0)),
            pl.BlockSpec((N_HEADS, d), lambda i: (0, 0)),
            pl.BlockSpec((1, N_HEADS), lambda i: (0, 0)),
        ],
        out_specs=pl.BlockSpec((tm, N_HEADS), lambda i: (i, 0)),
        out_shape=jax.ShapeDtypeStruct((m, N_HEADS), f32),
        compiler_params=_params("parallel"),
        name="logf_proj",
    )(x, g.reshape(1, d), w_ft, b_f.reshape(1, N_HEADS))


def _layer_norm_rows(v, g, b):
    mu = jnp.mean(v, axis=-1, keepdims=True)
    vc = v - mu
    var = jnp.mean(vc * vc, axis=-1, keepdims=True)
    return vc * lax.rsqrt(var + LN_EPS) * g + b


def _gmlp_mix_kernel(u_ref, v_ref, g_ref, b_ref, ws_ref, bs_ref, o_ref, *, n_groups, n_chunks):
    vn = _layer_norm_rows(v_ref[...], g_ref[...], b_ref[...]).astype(bf16)
    row = lax.broadcasted_iota(jnp.int32, (CHUNK, CHUNK), 0)
    col = lax.broadcasted_iota(jnp.int32, (CHUNK, CHUNK), 1)
    causal = row >= col
    for gi in range(n_groups):
        cols = slice(gi * GMLP_GROUP, (gi + 1) * GMLP_GROUP)
        ws = jnp.where(causal, ws_ref[gi], 0.0).astype(bf16)
        rhs = jnp.concatenate([vn[c * CHUNK:(c + 1) * CHUNK, cols] for c in range(n_chunks)], axis=1)
        mixed = jnp.dot(ws, rhs, preferred_element_type=f32) + bs_ref[:, gi:gi + 1]
        for c in range(n_chunks):
            rows = slice(c * CHUNK, (c + 1) * CHUNK)
            o_ref[rows, cols] = (u_ref[rows, cols] * mixed[:, c * GMLP_GROUP:(c + 1) * GMLP_GROUP]).astype(bf16)


def _gmlp_mix(z, ln_g, ln_b, w_s, b_s_t, *, n_chunks=2):
    m = z.shape[0]
    half = z.shape[1] // 2
    n_groups = half // GMLP_GROUP
    tm = n_chunks * CHUNK
    return pl.pallas_call(
        functools.partial(_gmlp_mix_kernel, n_groups=n_groups, n_chunks=n_chunks),
        grid=(m // tm,),
        in_specs=[
            pl.BlockSpec((tm, half), lambda i: (i, 0)),
            pl.BlockSpec((tm, half), lambda i: (i, 1)),
            pl.BlockSpec((1, half), lambda i: (0, 0)),
            pl.BlockSpec((1, half), lambda i: (0, 0)),
            pl.BlockSpec((n_groups, CHUNK, CHUNK), lambda i: (0, 0, 0)),
            pl.BlockSpec((CHUNK, n_groups), lambda i: (0, 0)),
        ],
        out_specs=pl.BlockSpec((tm, half), lambda i: (i, 0)),
        out_shape=jax.ShapeDtypeStruct((m, half), bf16),
        compiler_params=_params("parallel"),
        name="gmlp_mix",
    )(z, z, ln_g.reshape(1, half), ln_b.reshape(1, half), w_s, b_s_t)


def _gmlp_mix_first_row_kernel(u_ref, v_ref, g_ref, b_ref, w0_ref, b0_ref, o_ref, vn_ref):
    vn = _layer_norm_rows(v_ref[...], g_ref[...], b_ref[...])
    vn_ref[...] = vn
    mixed = vn * w0_ref[...] + b0_ref[...]
    o_ref[...] = (u_ref[...] * mixed).astype(bf16)


def _gmlp_mix_first_row(z, ln_g, ln_b, w00, b0):
    m = z.shape[0]
    half = z.shape[1] // 2
    row = lambda: pl.BlockSpec((1, half), lambda i: (0, 0))
    return pl.pallas_call(
        _gmlp_mix_first_row_kernel,
        grid=(1,),
        in_specs=[
            pl.BlockSpec((m, half), lambda i: (0, 0)),
            pl.BlockSpec((m, half), lambda i: (0, 1)),
            row(), row(), row(), row(),
        ],
        out_specs=[pl.BlockSpec((m, half), lambda i: (0, 0)), pl.BlockSpec((m, half), lambda i: (0, 0))],
        out_shape=[jax.ShapeDtypeStruct((m, half), bf16), jax.ShapeDtypeStruct((m, half), f32)],
        compiler_params=_params("arbitrary"),
        name="gmlp_mix_first_row",
    )(z, z, ln_g.reshape(1, half), ln_b.reshape(1, half), w00.reshape(1, half), b0.reshape(1, half))


def _cumsum_kernel(x_ref, o_ref, *, n_blocks):
    row = lax.broadcasted_iota(jnp.int32, (CHUNK, CHUNK), 0)
    col = lax.broadcasted_iota(jnp.int32, (CHUNK, CHUNK), 1)
    tri = (row >= col).astype(f32)
    carry = jnp.zeros((1, N_HEADS), f32)
    for blk in range(n_blocks):
        rows = slice(blk * CHUNK, (blk + 1) * CHUNK)
        cs = jnp.dot(tri, x_ref[0, rows, :], precision=HIGHEST, preferred_element_type=f32) + carry
        o_ref[0, rows, :] = cs
        carry = cs[CHUNK - 1:CHUNK, :]


def _cumsum_seq(lf):
    b, l, h = lf.shape
    return pl.pallas_call(
        functools.partial(_cumsum_kernel, n_blocks=l // CHUNK),
        grid=(b,),
        in_specs=[pl.BlockSpec((1, l, h), lambda i: (i, 0, 0))],
        out_specs=pl.BlockSpec((1, l, h), lambda i: (i, 0, 0)),
        out_shape=jax.ShapeDtypeStruct((b, l, h), f32),
        compiler_params=_params("parallel"),
        name="cumsum_seq",
    )(lf)


def _split3(x):
    hi = x.astype(bf16).astype(f32)
    r = x - hi
    mid = r.astype(bf16).astype(f32)
    lo = (r - mid).astype(bf16).astype(f32)
    return hi, mid, lo


def _flash_kernel(q_ref, k_ref, v_ref, c_ref, gate_ref, o_ref, kaug_ref, vaug_ref, qside_ref, s_ref, *, tq):
    h = pl.program_id(1)
    qi = pl.program_id(2)
    seq = k_ref.shape[1]

    @pl.when(qi == 0)
    def _():
        head = lax.broadcasted_iota(jnp.int32, (seq, N_HEADS), 1)
        c2 = jnp.sum(jnp.where(head == h, c_ref[0], 0.0), axis=1, keepdims=True) * LOG2E
        hi, mid, lo = _split3(c2)
        lane = lax.broadcasted_iota(jnp.int32, (seq, HEAD_DIM), 1)
        qside = jnp.where(lane == 0, hi, jnp.where(lane == 1, mid, jnp.where(lane == 2, lo,
                          jnp.where(lane < 6, 1.0, 0.0))))
        kside = jnp.where(lane < 3, 1.0, jnp.where(lane == 3, -hi, jnp.where(lane == 4, -mid,
                          jnp.where(lane == 5, -lo, 0.0))))
        qside_ref[...] = qside.astype(bf16)
        kaug_ref[:, :HEAD_DIM] = k_ref[0]
        kaug_ref[:, HEAD_DIM:] = kside.astype(bf16)
        vaug_ref[:, :HEAD_DIM] = v_ref[0]
        vaug_ref[:, HEAD_DIM:] = jnp.where(lane == 0, 1.0, 0.0).astype(bf16)

    q = jnp.concatenate([q_ref[0], qside_ref[pl.ds(pl.multiple_of(qi * tq, tq), tq), :]], axis=1)
    col_minus_row = (lax.broadcasted_iota(jnp.int32, (tq, tq), 1)
                     - lax.broadcasted_iota(jnp.int32, (tq, tq), 0))

    def key_rows(kj):
        return pl.ds(pl.multiple_of(jnp.minimum(kj, qi) * tq, tq), tq)

    def scores(kj):
        return lax.dot_general(q, kaug_ref[key_rows(kj), :], (((1,), (1,)), ((), ())),
                               preferred_element_type=f32)

    def absorb(kj, s, m, acc):
        m_new = jnp.maximum(m, jnp.max(s, axis=1, keepdims=True))
        p = jnp.exp2(s - m_new).astype(bf16)
        pv = jnp.dot(p, vaug_ref[key_rows(kj), :], preferred_element_type=f32)
        return m_new, jnp.exp2(m - m_new) * acc + pv

    s_ref[...] = scores(0)

    def pair(j, carry):
        m, acc = carry
        s_even = s_ref[...]
        s_odd = jnp.where(col_minus_row <= (qi - (2 * j + 1)) * tq, scores(2 * j + 1), -jnp.inf)
        m, acc = absorb(2 * j, s_even, m, acc)
        s_ref[...] = scores(2 * j + 2)
        return absorb(2 * j + 1, s_odd, m, acc)

    init = (jnp.full((tq, 1), -jnp.inf, f32), jnp.zeros((tq, 2 * HEAD_DIM), f32))
    m, acc = lax.fori_loop(0, (qi + 1) // 2, pair, init)

    def finish(acc):
        o = acc[:, :HEAD_DIM] / acc[:, HEAD_DIM:HEAD_DIM + 1]
        o_ref[0] = (o * gate_ref[0]).astype(bf16)

    @pl.when(qi % 2 == 1)
    def _():
        finish(acc)

    @pl.when(qi % 2 == 0)
    def _():
        s = jnp.where(col_minus_row <= 0, s_ref[...], -jnp.inf)
        finish(absorb(qi, s, m, acc)[1])


def _flash_prompt(q, k, v, c, gate, *, tq=512):
    b, l, d = q.shape
    blk = lambda: pl.BlockSpec((1, tq, HEAD_DIM), lambda bi, h, qi: (bi, qi, h))
    seq = lambda: pl.BlockSpec((1, l, HEAD_DIM), lambda bi, h, qi: (bi, 0, h))
    return pl.pallas_call(
        functools.partial(_flash_kernel, tq=tq),
        grid=(b, N_HEADS, l // tq),
        in_specs=[blk(), seq(), seq(), pl.BlockSpec((1, l, N_HEADS), lambda bi, h, qi: (bi, 0, 0)), blk()],
        out_specs=blk(),
        out_shape=jax.ShapeDtypeStruct((b, l, d), bf16),
        scratch_shapes=[pltpu.VMEM((l, 2 * HEAD_DIM), bf16), pltpu.VMEM((l, 2 * HEAD_DIM), bf16),
                        pltpu.VMEM((l, HEAD_DIM), bf16), pltpu.VMEM((tq, tq), f32)],
        compiler_params=_params("parallel", "parallel", "arbitrary"),
        name="flash_prompt",
    )(q, k, v, c, gate)


T_PER_ROW = LANES // N_HEADS
PAGES_PER_STEP = 4


def _eye(n, m):
    return lax.broadcasted_iota(jnp.int32, (n, m), 0) == lax.broadcasted_iota(jnp.int32, (n, m), 1)


def _lane_allreduce(x, op):
    sh = N_HEADS
    while sh < LANES:
        x = op(x, pltpu.roll(x, sh, 1))
        sh *= 2
    return x


def _flat_suffix(x):
    rows = x.shape[0]
    lane = lax.broadcasted_iota(jnp.int32, x.shape, 1)
    row = lax.broadcasted_iota(jnp.int32, x.shape, 0)
    y = x
    sh = N_HEADS
    while sh < LANES:
        y = y + jnp.where(lane < LANES - sh, pltpu.roll(y, LANES - sh, 1), 0.0)
        sh *= 2
    z = _lane_allreduce(jnp.where(lane < N_HEADS, y, 0.0), jnp.add)
    w = z
    sh = 1
    while sh < rows:
        w = w + jnp.where(row < rows - sh, pltpu.roll(w, rows - sh, 0), 0.0)
        sh *= 2
    return (y - x) + (w - z), w[0:1, :]


def _flat_scores(k3, q_rep):
    t = k3.shape[0]
    k2 = k3.reshape(t * N_HEADS, HEAD_DIM).astype(bf16)
    s = lax.dot_general(k2, q_rep, (((1,), (1,)), ((), ())), preferred_element_type=f32)
    s3 = s.reshape(t * N_HEADS // LANES, LANES, LANES)
    return jnp.sum(jnp.where(_eye(LANES, LANES)[None], s3, 0.0), axis=1)


def _dec_scores_kernel(pt_ref, q_ref, *refs, n_steps):
    g_pages = PAGES_PER_STEP
    k_refs, lf_refs = refs[:g_pages], refs[g_pages:2 * g_pages]
    knew_ref, lfnew_ref, s_ref, snew_ref, carry_ref = refs[2 * g_pages:]
    p = pl.program_id(1)

    @pl.when(p == 0)
    def _():
        carry_ref[...] = jnp.zeros_like(carry_ref)

    q_rep = q_ref[0]
    carry = carry_ref[...]
    for g in range(g_pages):
        later, total = _flat_suffix(lf_refs[g][0, 0])
        s_ref[0, g_pages - 1 - g] = _flat_scores(k_refs[g][0, 0], q_rep) + (later + carry + lfnew_ref[0])
        carry = carry + total
    carry_ref[...] = carry

    @pl.when(p == n_steps - 1)
    def _():
        s_new = lax.dot_general(knew_ref[0].astype(bf16), q_rep, (((1,), (1,)), ((), ())),
                                preferred_element_type=f32)
        s_new = jnp.sum(jnp.where(_eye(N_HEADS, LANES), s_new, 0.0), axis=0, keepdims=True)
        row = lax.broadcasted_iota(jnp.int32, snew_ref.shape[1:], 0)
        lane = lax.broadcasted_iota(jnp.int32, snew_ref.shape[1:], 1)
        snew_ref[0] = jnp.where((row == 0) & (lane < N_HEADS), s_new, -jnp.inf)


def _dec_scores(page_table, q_rep, cache_k, lf_flat, layer, k_new, lf_new_rep):
    nb, n_pages = page_table.shape
    page = cache_k.shape[2]
    g_pages = PAGES_PER_STEP
    n_steps = n_pages // g_pages
    rows = page // T_PER_ROW

    def past(g):
        return lambda b, p, pt: pt[b, n_pages - 1 - (p * g_pages + g)]

    k_specs = [pl.BlockSpec((1, 1, page, N_HEADS, HEAD_DIM),
                            lambda b, p, pt, f=past(g): (layer, f(b, p, pt), 0, 0, 0)) for g in range(g_pages)]
    lf_specs = [pl.BlockSpec((1, 1, rows, LANES),
                             lambda b, p, pt, f=past(g): (layer, f(b, p, pt), 0, 0)) for g in range(g_pages)]
    return pl.pallas_call(
        functools.partial(_dec_scores_kernel, n_steps=n_steps),
        grid_spec=pltpu.PrefetchScalarGridSpec(
            num_scalar_prefetch=1,
            grid=(nb, n_steps),
            in_specs=[pl.BlockSpec((1, LANES, HEAD_DIM), lambda b, p, pt: (b, 0, 0))] + k_specs + lf_specs + [
                pl.BlockSpec((1, N_HEADS, HEAD_DIM), lambda b, p, pt: (b, 0, 0)),
                pl.BlockSpec((1, 1, LANES), lambda b, p, pt: (b, 0, 0)),
            ],
            out_specs=[
                pl.BlockSpec((1, g_pages, rows, LANES), lambda b, p, pt: (b, n_steps - 1 - p, 0, 0)),
                pl.BlockSpec((1, SUBLANES, LANES), lambda b, p, pt: (b, 0, 0)),
            ],
            scratch_shapes=[pltpu.VMEM((1, LANES), f32)],
        ),
        out_shape=[jax.ShapeDtypeStruct((nb, n_pages, rows, LANES), f32),
                   jax.ShapeDtypeStruct((nb, SUBLANES, LANES), f32)],
        compiler_params=_params("parallel", "arbitrary"),
        name="dec_scores",
    )(page_table, q_rep, *([cache_k] * g_pages), *([lf_flat] * g_pages), k_new, lf_new_rep)


def _lane_broadcast_rows(flat):
    r = flat.shape[0]
    spread = jnp.where(_eye(LANES, LANES)[None], jnp.broadcast_to(flat[:, None, :], (r, LANES, LANES)), 0.0)
    ones = jnp.ones((LANES, HEAD_DIM), bf16)
    return jnp.dot(spread.reshape(r * LANES, LANES).astype(bf16), ones, preferred_element_type=f32)


def _dec_pv_kernel(pt_ref, s_ref, snew_ref, *refs, n_steps):
    g_pages = PAGES_PER_STEP
    v_refs = refs[:g_pages]
    vnew_ref, gate_ref, o_ref, p_ref, pnew_ref, acc_ref = refs[g_pages:]
    p = pl.program_id(1)

    @pl.when(p == 0)
    def _():
        s = s_ref[0]
        s_new = snew_ref[0]
        m = jnp.maximum(jnp.max(jnp.max(s, axis=0), axis=0, keepdims=True), jnp.max(s_new, axis=0, keepdims=True))
        m = _lane_allreduce(m, jnp.maximum)
        e = jnp.exp(s - m[None])
        e_new = jnp.exp(s_new - m)
        denom = jnp.sum(jnp.sum(e, axis=0), axis=0, keepdims=True) + jnp.sum(e_new, axis=0, keepdims=True)
        denom = _lane_allreduce(denom, jnp.add)
        p_ref[...] = e / denom[None]
        pnew_ref[...] = e_new / denom
        acc_ref[...] = jnp.zeros_like(acc_ref)

    acc = acc_ref[...]
    for g in range(g_pages):
        v3 = v_refs[g][0, 0]
        pb = _lane_broadcast_rows(p_ref[p * g_pages + g])
        acc = acc + jnp.sum(pb.reshape(v3.shape) * v3, axis=0)
    acc_ref[...] = acc

    @pl.when(p == n_steps - 1)
    def _():
        p_new = jnp.where(_eye(N_HEADS, LANES), jnp.broadcast_to(pnew_ref[0:1, :], (N_HEADS, LANES)), 0.0)
        pb = jnp.dot(p_new.astype(bf16), jnp.ones((LANES, HEAD_DIM), bf16), preferred_element_type=f32)
        o_ref[0] = (acc + pb * vnew_ref[0]) * gate_ref[0]


def _dec_pv(page_table, s_all, s_new, cache_v, layer, v_new, gate):
    nb, n_pages = page_table.shape
    page = cache_v.shape[2]
    g_pages = PAGES_PER_STEP
    n_steps = n_pages // g_pages
    rows = page // T_PER_ROW
    head_blk = lambda: pl.BlockSpec((1, N_HEADS, HEAD_DIM), lambda b, p, pt: (b, 0, 0))
    v_specs = [pl.BlockSpec((1, 1, page, N_HEADS, HEAD_DIM),
                            lambda b, p, pt, g=g: (layer, pt[b, p * g_pages + g], 0, 0, 0)) for g in range(g_pages)]
    return pl.pallas_call(
        functools.partial(_dec_pv_kernel, n_steps=n_steps),
        grid_spec=pltpu.PrefetchScalarGridSpec(
            num_scalar_prefetch=1,
            grid=(nb, n_steps),
            in_specs=[
                pl.BlockSpec((1, n_pages, rows, LANES), lambda b, p, pt: (b, 0, 0, 0)),
                pl.BlockSpec((1, SUBLANES, LANES), lambda b, p, pt: (b, 0, 0)),
            ] + v_specs + [head_blk(), head_blk()],
            out_specs=head_blk(),
            scratch_shapes=[pltpu.VMEM((n_pages, rows, LANES), f32), pltpu.VMEM((SUBLANES, LANES), f32),
                            pltpu.VMEM((N_HEADS, HEAD_DIM), f32)],
        ),
        out_shape=jax.ShapeDtypeStruct((nb, N_HEADS, HEAD_DIM), f32),
        compiler_params=_params("parallel", "arbitrary"),
        name="dec_pv",
    )(page_table, s_all, s_new, *([cache_v] * g_pages), v_new, gate)


def _final_norm_kernel(x_ref, g_ref, o_ref):
    o_ref[...] = _rms_rows(x_ref[...], g_ref[...])


def _final_norm(x, g, *, tm):
    m, d = x.shape
    return pl.pallas_call(
        _final_norm_kernel,
        grid=(m // tm,),
        in_specs=[pl.BlockSpec((tm, d), lambda i: (i, 0)), pl.BlockSpec((1, d), lambda i: (0, 0))],
        out_specs=pl.BlockSpec((tm, d), lambda i: (i, 0)),
        out_shape=jax.ShapeDtypeStruct((m, d), f32),
        compiler_params=_params("parallel"),
        name="final_norm",
    )(x, g.reshape(1, d))


def kernel(x_prompt, x_sample, cache_k, cache_v, cache_logf, page_table, norm_ffn1, ffn1_w_up, ffn1_w_down, norm_mix, norm_ffn2, ffn2_w_up, ffn2_w_down, gm_w_in, gm_ln_g, gm_ln_b, gm_w_s, gm_b_s, gm_w_out, fox_w_in, fox_b_f, fox_q_norm, fox_k_norm, fox_w_out, norm_final):
    bp, lp, d = x_prompt.shape
    bs, ls, _ = x_sample.shape
    assert ls == 1 and lp % 512 == 0 and d == N_HEADS * HEAD_DIM
    depth = norm_ffn1.shape[0]
    mp, ms = bp * lp, bs * ls
    tm_p = min(1024, mp)
    half = gm_w_in.shape[2] // 2

    xp = x_prompt.reshape(mp, d)
    xs = x_sample.reshape(ms, d)
    lf_p, lf_s, gv_s = [], [], []
    kp_all = vp_all = ks_all = vs_all = None
    n_fox = fox_w_in.shape[0]
    n_pool, page = cache_logf.shape[1:3]
    assert page == LANES and page_table.shape[1] % PAGES_PER_STEP == 0
    fox_w_t = jnp.swapaxes(fox_w_in, 1, 2)
    lf_flat = cache_logf.reshape(n_fox, n_pool, page // T_PER_ROW, LANES)

    for i in range(depth):
        xp = _ffn(xp, norm_ffn1[i], ffn1_w_up, ffn1_w_down, i, tm_up=tm_p, tm_down=512)
        xs = _ffn(xs, norm_ffn1[i], ffn1_w_up, ffn1_w_down, i, tm_up=ms, tm_down=ms)
        j = i // 2
        if i % 2 == 0:
            (zp,) = _rms_proj(xp, norm_mix[i], gm_w_in, j, col0=0, ncols=2 * half, mode="gelu", tm=tm_p)
            (zs,) = _rms_proj(xs, norm_mix[i], gm_w_in, j, col0=0, ncols=2 * half, mode="gelu", tm=ms)
            gp = _gmlp_mix(zp, gm_ln_g[j], gm_ln_b[j], gm_w_s[j], gm_b_s[j].T)
            w00 = jnp.repeat(gm_w_s[j, :, 0, 0], GMLP_GROUP)
            b0 = jnp.repeat(gm_b_s[j, :, 0], GMLP_GROUP)
            gs, gv = _gmlp_mix_first_row(zs, gm_ln_g[j], gm_ln_b[j], w00, b0)
            gv_s.append(gv.reshape(bs, ls, half))
            xp = _mm_res(gp, gm_w_out, j, xp, 1.0, tm=512)
            xs = _mm_res(gs, gm_w_out, j, xs, 1.0, tm=ms)
        else:
            w_ft = fox_w_t[j, 4 * d:, :]
            proj = functools.partial(_rms_proj, g=norm_mix[i], w=fox_w_t, layer=j, ncols=d, trans_w=True)
            (qp,) = proj(xp, col0=0, mode="q", tm=tm_p, gain=fox_q_norm[j], q_scale=SCORE_SCALE * LOG2E)
            kp_all, kp16 = proj(xp, col0=d, mode="k", tm=tm_p, gain=fox_k_norm[j], stack=(kp_all, j, n_fox))
            vp_all, vp16 = proj(xp, col0=2 * d, mode="v", tm=tm_p, stack=(vp_all, j, n_fox))
            (gatep,) = proj(xp, col0=3 * d, mode="gate", tm=tm_p)
            lfp = _logf_proj(xp, norm_mix[i], w_ft, fox_b_f[j], tm=tm_p).reshape(bp, lp, N_HEADS)
            og = _flash_prompt(qp.reshape(bp, lp, d), kp16.reshape(bp, lp, d), vp16.reshape(bp, lp, d),
                               _cumsum_seq(lfp), gatep.reshape(bp, lp, d))
            xp = _mm_res(og.reshape(mp, d), fox_w_out, j, xp, 1.0, tm=512)
            lf_p.append(lfp)
            (qs,) = proj(xs, col0=0, mode="q", tm=ms, gain=fox_q_norm[j], q_scale=SCORE_SCALE)
            ks_all, _ = proj(xs, col0=d, mode="k", tm=ms, gain=fox_k_norm[j], stack=(ks_all, j, n_fox))
            vs_all, _ = proj(xs, col0=2 * d, mode="v", tm=ms, stack=(vs_all, j, n_fox))
            (gates,) = proj(xs, col0=3 * d, mode="gate", tm=ms)
            lfs = _logf_proj(xs, norm_mix[i], w_ft, fox_b_f[j], tm=ms)
            q_rep = jnp.tile(qs.reshape(bs, N_HEADS, HEAD_DIM), (1, T_PER_ROW, 1))
            lf_new_rep = jnp.tile(lfs, (1, T_PER_ROW)).reshape(bs, 1, LANES)
            ks3 = ks_all[j].reshape(bs, N_HEADS, HEAD_DIM)
            vs3 = vs_all[j].reshape(bs, N_HEADS, HEAD_DIM)
            s_all, s_new = _dec_scores(page_table, q_rep, cache_k, lf_flat, j, ks3, lf_new_rep)
            os_ = _dec_pv(page_table, s_all, s_new, cache_v, j, vs3, gates.reshape(bs, N_HEADS, HEAD_DIM))
            xs = _mm_res(os_.reshape(ms, d).astype(bf16), fox_w_out, j, xs, 1.0, tm=ms)
            lf_s.append(lfs.reshape(bs, ls, N_HEADS))
        xp = _ffn(xp, norm_ffn2[i], ffn2_w_up, ffn2_w_down, i, tm_up=tm_p, tm_down=512)
        xs = _ffn(xs, norm_ffn2[i], ffn2_w_up, ffn2_w_down, i, tm_up=ms, tm_down=ms)

    y_prompt = _final_norm(xp, norm_final, tm=512).reshape(bp, lp, d)
    y_sample = _final_norm(xs, norm_final, tm=ms).reshape(bs, ls, d)
    heads_p = (n_fox, bp, lp, N_HEADS, HEAD_DIM)
    heads_s = (n_fox, bs, ls, N_HEADS, HEAD_DIM)
    return (y_prompt, y_sample, kp_all.reshape(heads_p), vp_all.reshape(heads_p), jnp.stack(lf_p),
            ks_all.reshape(heads_s), vs_all.reshape(heads_s), jnp.stack(lf_s), jnp.stack(gv_s))
```

```python
import functools

import jax
import jax.numpy as jnp
from jax import lax
from jax.experimental import pallas as pl
from jax.experimental.pallas import tpu as pltpu

f32 = jnp.float32
bf16 = jnp.bfloat16

N_HEADS = 16
HEAD_DIM = 128
CHUNK = 128
GMLP_GROUP = 128
NORM_EPS = 1e-6
LN_EPS = 1e-5
FFN_RESIDUAL = 0.5
LANES = 128
SUBLANES = 8

VMEM_LIMIT_BYTES = 56 * 1024 * 1024
HIGHEST = lax.Precision.HIGHEST
LOG2E = 1.4426950408889634
SCORE_SCALE = HEAD_DIM ** -0.5


def _params(*sem):
    return pltpu.CompilerParams(dimension_semantics=sem, vmem_limit_bytes=VMEM_LIMIT_BYTES)


def _rms_rows(x, g):
    ms = jnp.mean(x * x, axis=-1, keepdims=True)
    return x * lax.rsqrt(ms + NORM_EPS) * g


def _sigmoid(x):
    return 1.0 / (1.0 + jnp.exp(-x))


SAMPLE_ROWS = 16


def _ffn_up_kernel(x_ref, xs_ref, g_ref, wa_ref, wb_ref, o_ref, os_ref, xn_ref, *, nb, tn, nvalid, sub):
    j = pl.program_id(1)
    tm = x_ref.shape[0]
    ms = xs_ref.shape[0]

    @pl.when(j == 0)
    def _():
        xn_ref[:tm, :] = _rms_rows(x_ref[...], g_ref[...]).astype(bf16)
        xsn = _rms_rows(xs_ref[...], g_ref[...])
        pad = jnp.zeros((SAMPLE_ROWS - ms, xsn.shape[1]), f32)
        xn_ref[tm:, :] = jnp.concatenate([xsn, pad], axis=0).astype(bf16)

    def emit(ncols, b_off):
        xn = xn_ref[...]
        for c0 in range(0, ncols, sub):
            c1 = min(c0 + sub, ncols)
            a = jnp.dot(xn, wa_ref[0, :, c0:c1].astype(bf16), preferred_element_type=f32)
            b = jnp.dot(xn, wb_ref[0, :, b_off + c0:b_off + c1].astype(bf16), preferred_element_type=f32)
            h = (a * _sigmoid(a) * b).astype(o_ref.dtype)
            o_ref[:, c0:c1] = h[:tm]
            os_ref[:, c0:c1] = h[tm:]
        if ncols < tn:
            os_ref[:, ncols:] = jnp.zeros((SAMPLE_ROWS, tn - ncols), os_ref.dtype)

    if nvalid == tn:
        emit(tn, 0)
    else:
        @pl.when(j < nb - 1)
        def _():
            emit(tn, 0)

        @pl.when(j == nb - 1)
        def _():
            emit(nvalid, tn - nvalid)


def _ffn_up(x, xs, g, w_up, layer, *, tm, tn=512, sub=256):
    m, d = x.shape
    ms = xs.shape[0]
    assert ms <= SAMPLE_ROWS
    dff = w_up.shape[2] // 2
    nb = pl.cdiv(dff, tn)
    nvalid = dff - (nb - 1) * tn
    kern = functools.partial(_ffn_up_kernel, nb=nb, tn=tn, nvalid=nvalid, sub=sub)
    h, hs = pl.pallas_call(
        kern,
        grid=(m // tm, nb),
        in_specs=[
            pl.BlockSpec((tm, d), lambda i, j: (i, 0)),
            pl.BlockSpec((ms, d), lambda i, j: (0, 0)),
            pl.BlockSpec((1, d), lambda i, j: (0, 0)),
            pl.BlockSpec((1, d, tn), lambda i, j: (layer, 0, j)),
            pl.BlockSpec((pl.Element(1), pl.Element(d), pl.Element(tn)),
                         lambda i, j: (layer, 0, pl.multiple_of(jnp.minimum(dff + j * tn, 2 * dff - tn), LANES))),
        ],
        out_specs=[pl.BlockSpec((tm, tn), lambda i, j: (i, j)),
                   pl.BlockSpec((SAMPLE_ROWS, tn), lambda i, j: (0, jnp.where(i == 0, j, nb)))],
        out_shape=[jax.ShapeDtypeStruct((m, dff), bf16),
                   jax.ShapeDtypeStruct((SAMPLE_ROWS, (nb + 1) * tn), bf16)],
        scratch_shapes=[pltpu.VMEM((tm + SAMPLE_ROWS, d), bf16)],
        compiler_params=_params("arbitrary", "arbitrary"),
        name="ffn_up",
    )(x, xs, g.reshape(1, d), w_up, w_up)
    return h, hs[:ms, :dff]


def _mm_res_kernel(h_ref, w_ref, r_ref, o_ref, wb_ref, *, scale):
    @pl.when(pl.program_id(1) == 0)
    def _():
        wb_ref[...] = w_ref[0].astype(bf16)

    acc = jnp.dot(h_ref[...], wb_ref[...], preferred_element_type=f32)
    o_ref[...] = r_ref[...] + scale * acc


def _mm_res(h, w, layer, res, scale, *, tm, tn=512):
    m, k = h.shape
    n = w.shape[2]
    return pl.pallas_call(
        functools.partial(_mm_res_kernel, scale=scale),
        grid=(n // tn, m // tm),
        in_specs=[
            pl.BlockSpec((tm, k), lambda j, i: (i, 0)),
            pl.BlockSpec((1, k, tn), lambda j, i: (layer, 0, j)),
            pl.BlockSpec((tm, tn), lambda j, i: (i, j)),
        ],
        out_specs=pl.BlockSpec((tm, tn), lambda j, i: (i, j)),
        out_shape=jax.ShapeDtypeStruct((m, n), f32),
        scratch_shapes=[pltpu.VMEM((k, tn), bf16)],
        compiler_params=_params("parallel", "arbitrary"),
        name="mm_res",
    )(h, w, res)


def _ffn(x, xs, g, w_up, w_down, layer, *, tm_up, tm_down):
    h, hs = _ffn_up(x, xs, g, w_up, layer, tm=tm_up)
    x = _mm_res(h, w_down, layer, x, FFN_RESIDUAL, tm=tm_down)
    xs = _mm_res(hs, w_down, layer, xs, FFN_RESIDUAL, tm=xs.shape[0])
    return x, xs


def _head_rms(acc, gain, tn):
    outs = []
    for hh in range(tn // HEAD_DIM):
        blk = acc[:, hh * HEAD_DIM:(hh + 1) * HEAD_DIM]
        ms = jnp.mean(blk * blk, axis=-1, keepdims=True)
        outs.append(blk * lax.rsqrt(ms + NORM_EPS) * gain)
    return jnp.concatenate(outs, axis=1)


def _rms_proj_kernel(*refs, mode, tn, n_out, n_extra, trans_w, q_scale, qs_scale):
    x_ref, xs_ref, g_ref, w_ref = refs[:4]
    extra = refs[4:4 + n_extra]
    outs = refs[4 + n_extra:4 + n_extra + n_out]
    os_ref = refs[4 + n_extra + n_out]
    xn_ref = refs[-1]
    tm = x_ref.shape[0]
    ms = xs_ref.shape[0]

    @pl.when(pl.program_id(1) == 0)
    def _():
        xn_ref[:tm, :] = _rms_rows(x_ref[...], g_ref[...]).astype(bf16)
        xsn = _rms_rows(xs_ref[...], g_ref[...])
        pad = jnp.zeros((SAMPLE_ROWS - ms, xsn.shape[1]), f32)
        xn_ref[tm:, :] = jnp.concatenate([xsn, pad], axis=0).astype(bf16)

    if trans_w:
        acc = lax.dot_general(xn_ref[...], w_ref[0].astype(bf16), (((1,), (1,)), ((), ())),
                              preferred_element_type=f32)
    else:
        acc = jnp.dot(xn_ref[...], w_ref[0].astype(bf16), preferred_element_type=f32)
    if mode == "gelu":
        z = 0.5 * acc * (1.0 + lax.erf(acc * (0.5 ** 0.5)))
        outs[0][...] = z[:tm]
        os_ref[...] = z[tm:]
    elif mode == "q":
        qn = _head_rms(acc, extra[0][...], tn)
        outs[0][...] = (qn[:tm] * q_scale).astype(bf16)
        os_ref[...] = (qn[tm:] * qs_scale).astype(bf16)
    elif mode == "k":
        kn = _head_rms(acc, extra[0][...], tn)
        outs[0][...] = kn[:tm]
        outs[1][...] = kn[:tm].astype(bf16)
        os_ref[...] = kn[tm:]
    elif mode == "v":
        outs[0][...] = acc[:tm]
        outs[1][...] = acc[:tm].astype(bf16)
        os_ref[...] = acc[tm:]
    elif mode == "gate":
        gate = _sigmoid(acc)
        outs[0][...] = gate[:tm]
        os_ref[...] = gate[tm:]
    else:
        raise ValueError(mode)


def _rms_proj(x, xs, g, w, layer, *, col0, ncols, mode, tm, tn=512, gain=None, trans_w=False, stack=None,
              q_scale=None, qs_scale=None):
    m, d = x.shape
    ms = xs.shape[0]
    assert ms <= SAMPLE_ROWS
    out_dtypes = {"gelu": (f32,), "q": (bf16,), "k": (f32, bf16), "v": (f32, bf16), "gate": (f32,)}[mode]
    if trans_w:
        w_spec = pl.BlockSpec((1, tn, d), lambda i, j: (layer, col0 // tn + j, 0))
    else:
        w_spec = pl.BlockSpec((1, d, tn), lambda i, j: (layer, 0, col0 // tn + j))
    in_specs = [pl.BlockSpec((tm, d), lambda i, j: (i, 0)), pl.BlockSpec((ms, d), lambda i, j: (0, 0)),
                pl.BlockSpec((1, d), lambda i, j: (0, 0)), w_spec]
    args = [x, xs, g.reshape(1, d), w]
    n_fixed = len(args)
    if gain is not None:
        in_specs.append(pl.BlockSpec((1, HEAD_DIM), lambda i, j: (0, 0)))
        args.append(gain.reshape(1, HEAD_DIM))
    out_specs = [pl.BlockSpec((tm, tn), lambda i, j: (i, j)) for _ in out_dtypes]
    out_shape = [jax.ShapeDtypeStruct((m, ncols), dt) for dt in out_dtypes]
    aliases = {}
    if stack is not None:
        buf, slot, n_slots = stack
        out_specs[0] = pl.BlockSpec((None, tm, tn), lambda i, j: (slot, i, j))
        out_shape[0] = jax.ShapeDtypeStruct((n_slots, m, ncols), out_dtypes[0])
        if buf is not None:
            in_specs.append(pl.BlockSpec(memory_space=pl.ANY))
            args.append(buf)
            aliases = {len(args) - 1: 0}
    nbn = ncols // tn
    out_specs.append(pl.BlockSpec((SAMPLE_ROWS, tn), lambda i, j: (0, jnp.where(i == 0, j, nbn))))
    out_shape.append(jax.ShapeDtypeStruct((SAMPLE_ROWS, ncols + tn), out_dtypes[0]))
    *out, out_s = pl.pallas_call(
        functools.partial(_rms_proj_kernel, mode=mode, tn=tn, n_out=len(out_dtypes), n_extra=len(args) - n_fixed,
                          trans_w=trans_w, q_scale=q_scale, qs_scale=qs_scale),
        grid=(m // tm, ncols // tn),
        in_specs=in_specs,
        out_specs=out_specs,
        out_shape=out_shape,
        input_output_aliases=aliases,
        scratch_shapes=[pltpu.VMEM((tm + SAMPLE_ROWS, d), bf16)],
        compiler_params=_params("arbitrary", "arbitrary"),
        name="rms_proj_" + mode,
    )(*args)
    return (*out, out_s[:ms, :ncols])


def _logf_kernel(x_ref, g_ref, w_ref, b_ref, o_ref):
    xn = _rms_rows(x_ref[...], g_ref[...]).astype(bf16)
    z = lax.dot_general(xn, w_ref[...].astype(bf16), (((1,), (1,)), ((), ())),
                        preferred_element_type=f32) + b_ref[...]
    o_ref[...] = jnp.minimum(z, 0.0) - jnp.log1p(jnp.exp(-jnp.abs(z)))


def _logf_proj(x, g, w_ft, b_f, *, tm):
    m, d = x.shape
    return pl.pallas_call(
        _logf_kernel,
        grid=(m // tm,),
        in_specs=[
            pl.BlockSpec((tm, d), lambda i: (i, 0)),
            pl.BlockSpec((1, d), lambda i: (0, 0)),
            pl.BlockSpec((N_HEADS, d), lambda i: (0, 0)),
            pl.BlockSpec((1, N_HEADS), lambda i: (0, 0)),
        ],
        out_specs=pl.BlockSpec((tm, N_HEADS), lambda i: (i, 0)),
        out_shape=jax.ShapeDtypeStruct((m, N_HEADS), f32),
        compiler_params=_params("parallel"),
        name="logf_proj",
    )(x, g.reshape(1, d), w_ft, b_f.reshape(1, N_HEADS))


def _layer_norm_rows(v, g, b):
    mu = jnp.mean(v, axis=-1, keepdims=True)
    vc = v - mu
    var = jnp.mean(vc * vc, axis=-1, keepdims=True)
    return vc * lax.rsqrt(var + LN_EPS) * g + b


def _gmlp_mix_kernel(u_ref, v_ref, g_ref, b_ref, ws_ref, bs_ref, o_ref, *, n_groups, n_chunks):
    vn = _layer_norm_rows(v_ref[...], g_ref[...], b_ref[...]).astype(bf16)
    row = lax.broadcasted_iota(jnp.int32, (CHUNK, CHUNK), 0)
    col = lax.broadcasted_iota(jnp.int32, (CHUNK, CHUNK), 1)
    causal = row >= col
    for gi in range(n_groups):
        cols = slice(gi * GMLP_GROUP, (gi + 1) * GMLP_GROUP)
        ws = jnp.where(causal, ws_ref[gi], 0.0).astype(bf16)
        rhs = jnp.concatenate([vn[c * CHUNK:(c + 1) * CHUNK, cols] for c in range(n_chunks)], axis=1)
        mixed = jnp.dot(ws, rhs, preferred_element_type=f32) + bs_ref[:, gi:gi + 1]
        for c in range(n_chunks):
            rows = slice(c * CHUNK, (c + 1) * CHUNK)
            o_ref[rows, cols] = (u_ref[rows, cols] * mixed[:, c * GMLP_GROUP:(c + 1) * GMLP_GROUP]).astype(bf16)


def _gmlp_mix(z, ln_g, ln_b, w_s, b_s_t, *, n_chunks=2):
    m = z.shape[0]
    half = z.shape[1] // 2
    n_groups = half // GMLP_GROUP
    tm = n_chunks * CHUNK
    return pl.pallas_call(
        functools.partial(_gmlp_mix_kernel, n_groups=n_groups, n_chunks=n_chunks),
        grid=(m // tm,),
        in_specs=[
            pl.BlockSpec((tm, half), lambda i: (i, 0)),
            pl.BlockSpec((tm, half), lambda i: (i, 1)),
            pl.BlockSpec((1, half), lambda i: (0, 0)),
            pl.BlockSpec((1, half), lambda i: (0, 0)),
            pl.BlockSpec((n_groups, CHUNK, CHUNK), lambda i: (0, 0, 0)),
            pl.BlockSpec((CHUNK, n_groups), lambda i: (0, 0)),
        ],
        out_specs=pl.BlockSpec((tm, half), lambda i: (i, 0)),
        out_shape=jax.ShapeDtypeStruct((m, half), bf16),
        compiler_params=_params("parallel"),
        name="gmlp_mix",
    )(z, z, ln_g.reshape(1, half), ln_b.reshape(1, half), w_s, b_s_t)


def _gmlp_mix_first_row_kernel(u_ref, v_ref, g_ref, b_ref, w0_ref, b0_ref, o_ref, vn_ref):
    vn = _layer_norm_rows(v_ref[...], g_ref[...], b_ref[...])
    vn_ref[...] = vn
    mixed = vn * w0_ref[...] + b0_ref[...]
    o_ref[...] = (u_ref[...] * mixed).astype(bf16)


def _gmlp_mix_first_row(z, ln_g, ln_b, w00, b0):
    m = z.shape[0]
    half = z.shape[1] // 2
    row = lambda: pl.BlockSpec((1, half), lambda i: (0, 0))
    return pl.pallas_call(
        _gmlp_mix_first_row_kernel,
        grid=(1,),
        in_specs=[
            pl.BlockSpec((m, half), lambda i: (0, 0)),
            pl.BlockSpec((m, half), lambda i: (0, 1)),
            row(), row(), row(), row(),
        ],
        out_specs=[pl.BlockSpec((m, half), lambda i: (0, 0)), pl.BlockSpec((m, half), lambda i: (0, 0))],
        out_shape=[jax.ShapeDtypeStruct((m, half), bf16), jax.ShapeDtypeStruct((m, half), f32)],
        compiler_params=_params("arbitrary"),
        name="gmlp_mix_first_row",
    )(z, z, ln_g.reshape(1, half), ln_b.reshape(1, half), w00.reshape(1, half), b0.reshape(1, half))


def _cumsum_kernel(x_ref, o_ref, *, n_blocks):
    row = lax.broadcasted_iota(jnp.int32, (CHUNK, CHUNK), 0)
    col = lax.broadcasted_iota(jnp.int32, (CHUNK, CHUNK), 1)
    tri = (row >= col).astype(f32)
    carry = jnp.zeros((1, N_HEADS), f32)
    for blk in range(n_blocks):
        rows = slice(blk * CHUNK, (blk + 1) * CHUNK)
        cs = jnp.dot(tri, x_ref[0, rows, :], precision=HIGHEST, preferred_element_type=f32) + carry
        o_ref[0, rows, :] = cs
        carry = cs[CHUNK - 1:CHUNK, :]


def _cumsum_seq(lf):
    b, l, h = lf.shape
    return pl.pallas_call(
        functools.partial(_cumsum_kernel, n_blocks=l // CHUNK),
        grid=(b,),
        in_specs=[pl.BlockSpec((1, l, h), lambda i: (i, 0, 0))],
        out_specs=pl.BlockSpec((1, l, h), lambda i: (i, 0, 0)),
        out_shape=jax.ShapeDtypeStruct((b, l, h), f32),
        compiler_params=_params("parallel"),
        name="cumsum_seq",
    )(lf)


def _split3(x):
    hi = x.astype(bf16).astype(f32)
    r = x - hi
    mid = r.astype(bf16).astype(f32)
    lo = (r - mid).astype(bf16).astype(f32)
    return hi, mid, lo


def _flash_kernel(q_ref, k_ref, v_ref, c_ref, gate_ref, o_ref, kaug_ref, vaug_ref, qside_ref, s_ref, *, tq):
    h = pl.program_id(1)
    qi = pl.program_id(2)
    seq = k_ref.shape[1]

    @pl.when(qi == 0)
    def _():
        head = lax.broadcasted_iota(jnp.int32, (seq, N_HEADS), 1)
        c2 = jnp.sum(jnp.where(head == h, c_ref[0], 0.0), axis=1, keepdims=True) * LOG2E
        hi, mid, lo = _split3(c2)
        lane = lax.broadcasted_iota(jnp.int32, (seq, HEAD_DIM), 1)
        qside = jnp.where(lane == 0, hi, jnp.where(lane == 1, mid, jnp.where(lane == 2, lo,
                          jnp.where(lane < 6, 1.0, 0.0))))
        kside = jnp.where(lane < 3, 1.0, jnp.where(lane == 3, -hi, jnp.where(lane == 4, -mid,
                          jnp.where(lane == 5, -lo, 0.0))))
        qside_ref[...] = qside.astype(bf16)
        kaug_ref[:, :HEAD_DIM] = k_ref[0]
        kaug_ref[:, HEAD_DIM:] = kside.astype(bf16)
        vaug_ref[:, :HEAD_DIM] = v_ref[0]
        vaug_ref[:, HEAD_DIM:] = jnp.where(lane == 0, 1.0, 0.0).astype(bf16)

    q = jnp.concatenate([q_ref[0], qside_ref[pl.ds(pl.multiple_of(qi * tq, tq), tq), :]], axis=1)
    col_minus_row = (lax.broadcasted_iota(jnp.int32, (tq, tq), 1)
                     - lax.broadcasted_iota(jnp.int32, (tq, tq), 0))

    def key_rows(kj):
        return pl.ds(pl.multiple_of(jnp.minimum(kj, qi) * tq, tq), tq)

    def scores(kj):
        return lax.dot_general(q, kaug_ref[key_rows(kj), :], (((1,), (1,)), ((), ())),
                               preferred_element_type=f32)

    def absorb(kj, s, m, acc):
        m_new = jnp.maximum(m, jnp.max(s, axis=1, keepdims=True))
        p = jnp.exp2(s - m_new).astype(bf16)
        pv = jnp.dot(p, vaug_ref[key_rows(kj), :], preferred_element_type=f32)
        return m_new, jnp.exp2(m - m_new) * acc + pv

    s_ref[...] = scores(0)

    def pair(j, carry):
        m, acc = carry
        s_even = s_ref[...]
        s_odd = jnp.where(col_minus_row <= (qi - (2 * j + 1)) * tq, scores(2 * j + 1), -jnp.inf)
        m, acc = absorb(2 * j, s_even, m, acc)
        s_ref[...] = scores(2 * j + 2)
        return absorb(2 * j + 1, s_odd, m, acc)

    init = (jnp.full((tq, 1), -jnp.inf, f32), jnp.zeros((tq, 2 * HEAD_DIM), f32))
    m, acc = lax.fori_loop(0, (qi + 1) // 2, pair, init)

    def finish(acc):
        o = acc[:, :HEAD_DIM] / acc[:, HEAD_DIM:HEAD_DIM + 1]
        o_ref[0] = (o * gate_ref[0]).astype(bf16)

    @pl.when(qi % 2 == 1)
    def _():
        finish(acc)

    @pl.when(qi % 2 == 0)
    def _():
        s = jnp.where(col_minus_row <= 0, s_ref[...], -jnp.inf)
        finish(absorb(qi, s, m, acc)[1])


def _flash_prompt(q, k, v, c, gate, *, tq=512):
    b, l, d = q.shape
    blk = lambda: pl.BlockSpec((1, tq, HEAD_DIM), lambda bi, h, qi: (bi, qi, h))
    seq = lambda: pl.BlockSpec((1, l, HEAD_DIM), lambda bi, h, qi: (bi, 0, h))
    return pl.pallas_call(
        functools.partial(_flash_kernel, tq=tq),
        grid=(b, N_HEADS, l // tq),
        in_specs=[blk(), seq(), seq(), pl.BlockSpec((1, l, N_HEADS), lambda bi, h, qi: (bi, 0, 0)), blk()],
        out_specs=blk(),
        out_shape=jax.ShapeDtypeStruct((b, l, d), bf16),
        scratch_shapes=[pltpu.VMEM((l, 2 * HEAD_DIM), bf16), pltpu.VMEM((l, 2 * HEAD_DIM), bf16),
                        pltpu.VMEM((l, HEAD_DIM), bf16), pltpu.VMEM((tq, tq), f32)],
        compiler_params=_params("parallel", "parallel", "arbitrary"),
        name="flash_prompt",
    )(q, k, v, c, gate)


T_PER_ROW = LANES // N_HEADS
PAGES_PER_STEP = 8


def _eye(n, m):
    return lax.broadcasted_iota(jnp.int32, (n, m), 0) == lax.broadcasted_iota(jnp.int32, (n, m), 1)


def _lane_allreduce(x, op):
    sh = N_HEADS
    while sh < LANES:
        x = op(x, pltpu.roll(x, sh, 1))
        sh *= 2
    return x


def _flat_suffix(x):
    rows = x.shape[0]
    lane = lax.broadcasted_iota(jnp.int32, x.shape, 1)
    row = lax.broadcasted_iota(jnp.int32, x.shape, 0)
    y = x
    sh = N_HEADS
    while sh < LANES:
        y = y + jnp.where(lane < LANES - sh, pltpu.roll(y, LANES - sh, 1), 0.0)
        sh *= 2
    z = _lane_allreduce(jnp.where(lane < N_HEADS, y, 0.0), jnp.add)
    w = z
    sh = 1
    while sh < rows:
        w = w + jnp.where(row < rows - sh, pltpu.roll(w, rows - sh, 0), 0.0)
        sh *= 2
    return (y - x) + (w - z), w[0:1, :]


def _flat_scores(k3, q_rep):
    t = k3.shape[0]
    k2 = k3.reshape(t * N_HEADS, HEAD_DIM).astype(bf16)
    s = lax.dot_general(k2, q_rep, (((1,), (1,)), ((), ())), preferred_element_type=f32)
    s3 = s.reshape(t * N_HEADS // LANES, LANES, LANES)
    return jnp.sum(jnp.where(_eye(LANES, LANES)[None], s3, 0.0), axis=1)


def _dec_scores_kernel(pt_ref, q_ref, *refs, n_steps):
    g_pages = PAGES_PER_STEP
    k_refs, lf_refs = refs[:g_pages], refs[g_pages:2 * g_pages]
    knew_ref, lfnew_ref, s_ref, snew_ref, carry_ref = refs[2 * g_pages:]
    p = pl.program_id(1)

    @pl.when(p == 0)
    def _():
        carry_ref[...] = jnp.zeros_like(carry_ref)

    q_rep = q_ref[0]
    carry = carry_ref[...]
    for g in range(g_pages):
        later, total = _flat_suffix(lf_refs[g][0, 0])
        s_ref[0, g_pages - 1 - g] = _flat_scores(k_refs[g][0, 0], q_rep) + (later + carry + lfnew_ref[0])
        carry = carry + total
    carry_ref[...] = carry

    @pl.when(p == n_steps - 1)
    def _():
        s_new = lax.dot_general(knew_ref[0].astype(bf16), q_rep, (((1,), (1,)), ((), ())),
                                preferred_element_type=f32)
        s_new = jnp.sum(jnp.where(_eye(N_HEADS, LANES), s_new, 0.0), axis=0, keepdims=True)
        row = lax.broadcasted_iota(jnp.int32, snew_ref.shape[1:], 0)
        lane = lax.broadcasted_iota(jnp.int32, snew_ref.shape[1:], 1)
        snew_ref[0] = jnp.where((row == 0) & (lane < N_HEADS), s_new, -jnp.inf)


def _dec_scores(page_table, q_rep, cache_k, lf_flat, layer, k_new, lf_new_rep):
    nb, n_pages = page_table.shape
    page = cache_k.shape[2]
    g_pages = PAGES_PER_STEP
    n_steps = n_pages // g_pages
    rows = page // T_PER_ROW

    def past(g):
        return lambda b, p, pt: pt[b, n_pages - 1 - (p * g_pages + g)]

    k_specs = [pl.BlockSpec((1, 1, page, N_HEADS, HEAD_DIM),
                            lambda b, p, pt, f=past(g): (layer, f(b, p, pt), 0, 0, 0)) for g in range(g_pages)]
    lf_specs = [pl.BlockSpec((1, 1, rows, LANES),
                             lambda b, p, pt, f=past(g): (layer, f(b, p, pt), 0, 0)) for g in range(g_pages)]
    return pl.pallas_call(
        functools.partial(_dec_scores_kernel, n_steps=n_steps),
        grid_spec=pltpu.PrefetchScalarGridSpec(
            num_scalar_prefetch=1,
            grid=(nb, n_steps),
            in_specs=[pl.BlockSpec((1, LANES, HEAD_DIM), lambda b, p, pt: (b, 0, 0))] + k_specs + lf_specs + [
                pl.BlockSpec((1, N_HEADS, HEAD_DIM), lambda b, p, pt: (b, 0, 0)),
                pl.BlockSpec((1, 1, LANES), lambda b, p, pt: (b, 0, 0)),
            ],
            out_specs=[
                pl.BlockSpec((1, g_pages, rows, LANES), lambda b, p, pt: (b, n_steps - 1 - p, 0, 0)),
                pl.BlockSpec((1, SUBLANES, LANES), lambda b, p, pt: (b, 0, 0)),
            ],
            scratch_shapes=[pltpu.VMEM((1, LANES), f32)],
        ),
        out_shape=[jax.ShapeDtypeStruct((nb, n_pages, rows, LANES), f32),
                   jax.ShapeDtypeStruct((nb, SUBLANES, LANES), f32)],
        compiler_params=_params("parallel", "arbitrary"),
        name="dec_scores",
    )(page_table, q_rep, *([cache_k] * g_pages), *([lf_flat] * g_pages), k_new, lf_new_rep)


def _lane_broadcast_rows(flat):
    r = flat.shape[0]
    spread = jnp.where(_eye(LANES, LANES)[None], jnp.broadcast_to(flat[:, None, :], (r, LANES, LANES)), 0.0)
    ones = jnp.ones((LANES, HEAD_DIM), bf16)
    return jnp.dot(spread.reshape(r * LANES, LANES).astype(bf16), ones, preferred_element_type=f32)


def _dec_pv_kernel(pt_ref, s_ref, snew_ref, *refs, n_steps):
    g_pages = PAGES_PER_STEP
    v_refs = refs[:g_pages]
    vnew_ref, gate_ref, o_ref, p_ref, pnew_ref, acc_ref = refs[g_pages:]
    p = pl.program_id(1)

    @pl.when(p == 0)
    def _():
        s = s_ref[0]
        s_new = snew_ref[0]
        m = jnp.maximum(jnp.max(jnp.max(s, axis=0), axis=0, keepdims=True), jnp.max(s_new, axis=0, keepdims=True))
        m = _lane_allreduce(m, jnp.maximum)
        e = jnp.exp(s - m[None])
        e_new = jnp.exp(s_new - m)
        denom = jnp.sum(jnp.sum(e, axis=0), axis=0, keepdims=True) + jnp.sum(e_new, axis=0, keepdims=True)
        denom = _lane_allreduce(denom, jnp.add)
        p_ref[...] = e / denom[None]
        pnew_ref[...] = e_new / denom
        acc_ref[...] = jnp.zeros_like(acc_ref)

    acc = acc_ref[...]
    for g in range(g_pages):
        v3 = v_refs[g][0, 0]
        pb = _lane_broadcast_rows(p_ref[p * g_pages + g])
        acc = acc + jnp.sum(pb.reshape(v3.shape) * v3, axis=0)
    acc_ref[...] = acc

    @pl.when(p == n_steps - 1)
    def _():
        p_new = jnp.where(_eye(N_HEADS, LANES), jnp.broadcast_to(pnew_ref[0:1, :], (N_HEADS, LANES)), 0.0)
        pb = jnp.dot(p_new.astype(bf16), jnp.ones((LANES, HEAD_DIM), bf16), preferred_element_type=f32)
        o_ref[0] = (acc + pb * vnew_ref[0]) * gate_ref[0]


def _dec_pv(page_table, s_all, s_new, cache_v, layer, v_new, gate):
    nb, n_pages = page_table.shape
    page = cache_v.shape[2]
    g_pages = PAGES_PER_STEP
    n_steps = n_pages // g_pages
    rows = page // T_PER_ROW
    head_blk = lambda: pl.BlockSpec((1, N_HEADS, HEAD_DIM), lambda b, p, pt: (b, 0, 0))
    v_specs = [pl.BlockSpec((1, 1, page, N_HEADS, HEAD_DIM),
                            lambda b, p, pt, g=g: (layer, pt[b, p * g_pages + g], 0, 0, 0)) for g in range(g_pages)]
    return pl.pallas_call(
        functools.partial(_dec_pv_kernel, n_steps=n_steps),
        grid_spec=pltpu.PrefetchScalarGridSpec(
            num_scalar_prefetch=1,
            grid=(nb, n_steps),
            in_specs=[
                pl.BlockSpec((1, n_pages, rows, LANES), lambda b, p, pt: (b, 0, 0, 0)),
                pl.BlockSpec((1, SUBLANES, LANES), lambda b, p, pt: (b, 0, 0)),
            ] + v_specs + [head_blk(), head_blk()],
            out_specs=head_blk(),
            scratch_shapes=[pltpu.VMEM((n_pages, rows, LANES), f32), pltpu.VMEM((SUBLANES, LANES), f32),
                            pltpu.VMEM((N_HEADS, HEAD_DIM), f32)],
        ),
        out_shape=jax.ShapeDtypeStruct((nb, N_HEADS, HEAD_DIM), f32),
        compiler_params=_params("parallel", "arbitrary"),
        name="dec_pv",
    )(page_table, s_all, s_new, *([cache_v] * g_pages), v_new, gate)


def _final_norm_kernel(x_ref, g_ref, o_ref):
    o_ref[...] = _rms_rows(x_ref[...], g_ref[...])


def _final_norm(x, g, *, tm):
    m, d = x.shape
    return pl.pallas_call(
        _final_norm_kernel,
        grid=(m // tm,),
        in_specs=[pl.BlockSpec((tm, d), lambda i: (i, 0)), pl.BlockSpec((1, d), lambda i: (0, 0))],
        out_specs=pl.BlockSpec((tm, d), lambda i: (i, 0)),
        out_shape=jax.ShapeDtypeStruct((m, d), f32),
        compiler_params=_params("parallel"),
        name="final_norm",
    )(x, g.reshape(1, d))


def kernel(x_prompt, x_sample, cache_k, cache_v, cache_logf, page_table, norm_ffn1, ffn1_w_up, ffn1_w_down, norm_mix, norm_ffn2, ffn2_w_up, ffn2_w_down, gm_w_in, gm_ln_g, gm_ln_b, gm_w_s, gm_b_s, gm_w_out, fox_w_in, fox_b_f, fox_q_norm, fox_k_norm, fox_w_out, norm_final):
    bp, lp, d = x_prompt.shape
    bs, ls, _ = x_sample.shape
    assert ls == 1 and lp % 512 == 0 and d == N_HEADS * HEAD_DIM
    depth = norm_ffn1.shape[0]
    mp, ms = bp * lp, bs * ls
    tm_p = min(1024, mp)
    half = gm_w_in.shape[2] // 2

    xp = x_prompt.reshape(mp, d)
    xs = x_sample.reshape(ms, d)
    lf_p, k_s, v_s, lf_s, gv_s = [], [], [], [], []
    kp_all = vp_all = None
    n_fox = fox_w_in.shape[0]
    n_pool, page = cache_logf.shape[1:3]
    assert page == LANES and page_table.shape[1] % PAGES_PER_STEP == 0
    fox_w_t = jnp.swapaxes(fox_w_in, 1, 2)
    lf_flat = cache_logf.reshape(n_fox, n_pool, page // T_PER_ROW, LANES)

    for i in range(depth):
        xp, xs = _ffn(xp, xs, norm_ffn1[i], ffn1_w_up, ffn1_w_down, i, tm_up=tm_p, tm_down=512)
        j = i // 2
        if i % 2 == 0:
            zp, zs = _rms_proj(xp, xs, norm_mix[i], gm_w_in, j, col0=0, ncols=2 * half, mode="gelu", tm=tm_p)
            gp = _gmlp_mix(zp, gm_ln_g[j], gm_ln_b[j], gm_w_s[j], gm_b_s[j].T)
            w00 = jnp.repeat(gm_w_s[j, :, 0, 0], GMLP_GROUP)
            b0 = jnp.repeat(gm_b_s[j, :, 0], GMLP_GROUP)
            gs, gv = _gmlp_mix_first_row(zs, gm_ln_g[j], gm_ln_b[j], w00, b0)
            gv_s.append(gv.reshape(bs, ls, half))
            xp = _mm_res(gp, gm_w_out, j, xp, 1.0, tm=512)
            xs = _mm_res(gs, gm_w_out, j, xs, 1.0, tm=ms)
        else:
            w_ft = fox_w_t[j, 4 * d:, :]
            proj = functools.partial(_rms_proj, xp, xs, norm_mix[i], fox_w_t, j, ncols=d, tm=tm_p, trans_w=True)
            qp, qs = proj(col0=0, mode="q", gain=fox_q_norm[j], q_scale=SCORE_SCALE * LOG2E, qs_scale=SCORE_SCALE)
            kp_all, kp16, ks = proj(col0=d, mode="k", gain=fox_k_norm[j], stack=(kp_all, j, n_fox))
            vp_all, vp16, vs = proj(col0=2 * d, mode="v", stack=(vp_all, j, n_fox))
            gatep, gates = proj(col0=3 * d, mode="gate")
            lfp = _logf_proj(xp, norm_mix[i], w_ft, fox_b_f[j], tm=tm_p).reshape(bp, lp, N_HEADS)
            og = _flash_prompt(qp.reshape(bp, lp, d), kp16.reshape(bp, lp, d), vp16.reshape(bp, lp, d),
                               _cumsum_seq(lfp), gatep.reshape(bp, lp, d))
            xp = _mm_res(og.reshape(mp, d), fox_w_out, j, xp, 1.0, tm=512)
            lf_p.append(lfp)
            lfs = _logf_proj(xs, norm_mix[i], w_ft, fox_b_f[j], tm=ms)
            q_rep = jnp.tile(qs.reshape(bs, N_HEADS, HEAD_DIM), (1, T_PER_ROW, 1))
            lf_new_rep = jnp.tile(lfs, (1, T_PER_ROW)).reshape(bs, 1, LANES)
            ks3 = ks.reshape(bs, N_HEADS, HEAD_DIM)
            vs3 = vs.reshape(bs, N_HEADS, HEAD_DIM)
            k_s.append(ks3.reshape(bs, ls, N_HEADS, HEAD_DIM))
            v_s.append(vs3.reshape(bs, ls, N_HEADS, HEAD_DIM))
            s_all, s_new = _dec_scores(page_table, q_rep, cache_k, lf_flat, j, ks3, lf_new_rep)
            os_ = _dec_pv(page_table, s_all, s_new, cache_v, j, vs3, gates.reshape(bs, N_HEADS, HEAD_DIM))
            xs = _mm_res(os_.reshape(ms, d).astype(bf16), fox_w_out, j, xs, 1.0, tm=ms)
            lf_s.append(lfs.reshape(bs, ls, N_HEADS))
        xp, xs = _ffn(xp, xs, norm_ffn2[i], ffn2_w_up, ffn2_w_down, i, tm_up=tm_p, tm_down=512)

    y_prompt = _final_norm(xp, norm_final, tm=512).reshape(bp, lp, d)
    y_sample = _final_norm(xs, norm_final, tm=ms).reshape(bs, ls, d)
    heads_p = (n_fox, bp, lp, N_HEADS, HEAD_DIM)
    return (y_prompt, y_sample, kp_all.reshape(heads_p), vp_all.reshape(heads_p), jnp.stack(lf_p),
            jnp.stack(k_s), jnp.stack(v_s), jnp.stack(lf_s), jnp.stack(gv_s))
```

```python
import functools

import jax
import jax.numpy as jnp
from jax import lax
from jax.experimental import pallas as pl
from jax.experimental.pallas import tpu as pltpu

f32 = jnp.float32
bf16 = jnp.bfloat16

N_HEADS = 16
HEAD_DIM = 128
CHUNK = 128
GMLP_GROUP = 128
NORM_EPS = 1e-6
LN_EPS = 1e-5
FFN_RESIDUAL = 0.5
LANES = 128
SUBLANES = 8

VMEM_LIMIT_BYTES = 56 * 1024 * 1024
HIGHEST = lax.Precision.HIGHEST
LOG2E = 1.4426950408889634
SCORE_SCALE = HEAD_DIM ** -0.5


def _params(*sem):
    return pltpu.CompilerParams(dimension_semantics=sem, vmem_limit_bytes=VMEM_LIMIT_BYTES)


def _rms_rows(x, g):
    ms = jnp.mean(x * x, axis=-1, keepdims=True)
    return x * lax.rsqrt(ms + NORM_EPS) * g


def _sigmoid(x):
    return 1.0 / (1.0 + jnp.exp(-x))


SAMPLE_ROWS = 16


def _ffn_up_kernel(x_ref, xs_ref, g_ref, wa_ref, wb_ref, o_ref, os_ref, xn_ref, *, nb, tn, nvalid, sub):
    j = pl.program_id(1)
    tm = x_ref.shape[0]
    ms = xs_ref.shape[0]

    @pl.when(j == 0)
    def _():
        xn_ref[:tm, :] = _rms_rows(x_ref[...], g_ref[...]).astype(bf16)
        xsn = _rms_rows(xs_ref[...], g_ref[...])
        pad = jnp.zeros((SAMPLE_ROWS - ms, xsn.shape[1]), f32)
        xn_ref[tm:, :] = jnp.concatenate([xsn, pad], axis=0).astype(bf16)

    def emit(ncols, b_off):
        xn = xn_ref[...]
        for c0 in range(0, ncols, sub):
            c1 = min(c0 + sub, ncols)
            a = jnp.dot(xn, wa_ref[0, :, c0:c1].astype(bf16), preferred_element_type=f32)
            b = jnp.dot(xn, wb_ref[0, :, b_off + c0:b_off + c1].astype(bf16), preferred_element_type=f32)
            h = (a * _sigmoid(a) * b).astype(o_ref.dtype)
            o_ref[:, c0:c1] = h[:tm]
            os_ref[:, c0:c1] = h[tm:]
        if ncols < tn:
            os_ref[:, ncols:] = jnp.zeros((SAMPLE_ROWS, tn - ncols), os_ref.dtype)

    if nvalid == tn:
        emit(tn, 0)
    else:
        @pl.when(j < nb - 1)
        def _():
            emit(tn, 0)

        @pl.when(j == nb - 1)
        def _():
            emit(nvalid, tn - nvalid)


def _ffn_up(x, xs, g, w_up, layer, *, tm, tn=512, sub=256):
    m, d = x.shape
    ms = xs.shape[0]
    assert ms <= SAMPLE_ROWS
    dff = w_up.shape[2] // 2
    nb = pl.cdiv(dff, tn)
    nvalid = dff - (nb - 1) * tn
    kern = functools.partial(_ffn_up_kernel, nb=nb, tn=tn, nvalid=nvalid, sub=sub)
    h, hs = pl.pallas_call(
        kern,
        grid=(m // tm, nb),
        in_specs=[
            pl.BlockSpec((tm, d), lambda i, j: (i, 0)),
            pl.BlockSpec((ms, d), lambda i, j: (0, 0)),
            pl.BlockSpec((1, d), lambda i, j: (0, 0)),
            pl.BlockSpec((1, d, tn), lambda i, j: (layer, 0, j)),
            pl.BlockSpec((pl.Element(1), pl.Element(d), pl.Element(tn)),
                         lambda i, j: (layer, 0, pl.multiple_of(jnp.minimum(dff + j * tn, 2 * dff - tn), LANES))),
        ],
        out_specs=[pl.BlockSpec((tm, tn), lambda i, j: (i, j)),
                   pl.BlockSpec((SAMPLE_ROWS, tn), lambda i, j: (0, jnp.where(i == 0, j, nb)))],
        out_shape=[jax.ShapeDtypeStruct((m, dff), bf16),
                   jax.ShapeDtypeStruct((SAMPLE_ROWS, (nb + 1) * tn), bf16)],
        scratch_shapes=[pltpu.VMEM((tm + SAMPLE_ROWS, d), bf16)],
        compiler_params=_params("arbitrary", "arbitrary"),
        name="ffn_up",
    )(x, xs, g.reshape(1, d), w_up, w_up)
    return h, hs[:ms, :dff]


def _mm_res_kernel(h_ref, hs_ref, w_ref, r_ref, rs_ref, o_ref, os_ref, wb_ref, *, scale):
    @pl.when(pl.program_id(1) == 0)
    def _():
        wb_ref[...] = w_ref[0].astype(bf16)
        os_ref[...] = rs_ref[...] + scale * jnp.dot(hs_ref[...], wb_ref[...], preferred_element_type=f32)

    acc = jnp.dot(h_ref[...], wb_ref[...], preferred_element_type=f32)
    o_ref[...] = r_ref[...] + scale * acc


def _mm_res(h, hs, w, layer, res, res_s, scale, *, tm, tn=512):
    m, k = h.shape
    ms = hs.shape[0]
    n = w.shape[2]
    return pl.pallas_call(
        functools.partial(_mm_res_kernel, scale=scale),
        grid=(n // tn, m // tm),
        in_specs=[
            pl.BlockSpec((tm, k), lambda j, i: (i, 0)),
            pl.BlockSpec((ms, k), lambda j, i: (0, 0)),
            pl.BlockSpec((1, k, tn), lambda j, i: (layer, 0, j)),
            pl.BlockSpec((tm, tn), lambda j, i: (i, j)),
            pl.BlockSpec((ms, tn), lambda j, i: (0, j)),
        ],
        out_specs=[pl.BlockSpec((tm, tn), lambda j, i: (i, j)), pl.BlockSpec((ms, tn), lambda j, i: (0, j))],
        out_shape=[jax.ShapeDtypeStruct((m, n), f32), jax.ShapeDtypeStruct((ms, n), f32)],
        scratch_shapes=[pltpu.VMEM((k, tn), bf16)],
        compiler_params=_params("parallel", "arbitrary"),
        name="mm_res",
    )(h, hs, w, res, res_s)


def _ffn(x, xs, g, w_up, w_down, layer, *, tm_up, tm_down):
    h, hs = _ffn_up(x, xs, g, w_up, layer, tm=tm_up)
    return _mm_res(h, hs, w_down, layer, x, xs, FFN_RESIDUAL, tm=tm_down)


def _head_rms(acc, gain, tn):
    outs = []
    for hh in range(tn // HEAD_DIM):
        blk = acc[:, hh * HEAD_DIM:(hh + 1) * HEAD_DIM]
        ms = jnp.mean(blk * blk, axis=-1, keepdims=True)
        outs.append(blk * lax.rsqrt(ms + NORM_EPS) * gain)
    return jnp.concatenate(outs, axis=1)


def _gelu_proj_kernel(x_ref, xs_ref, g_ref, w_ref, o_ref, os_ref, xn_ref):
    tm = x_ref.shape[0]
    ms = xs_ref.shape[0]

    @pl.when(pl.program_id(1) == 0)
    def _():
        xn_ref[:tm, :] = _rms_rows(x_ref[...], g_ref[...]).astype(bf16)
        xsn = _rms_rows(xs_ref[...], g_ref[...])
        pad = jnp.zeros((SAMPLE_ROWS - ms, xsn.shape[1]), f32)
        xn_ref[tm:, :] = jnp.concatenate([xsn, pad], axis=0).astype(bf16)

    acc = jnp.dot(xn_ref[...], w_ref[0].astype(bf16), preferred_element_type=f32)
    z = 0.5 * acc * (1.0 + lax.erf(acc * (0.5 ** 0.5)))
    o_ref[...] = z[:tm].astype(o_ref.dtype)
    os_ref[...] = z[tm:]


def _gelu_proj(x, xs, g, w, layer, *, tm, tn=512):
    m, d = x.shape
    ms = xs.shape[0]
    assert ms <= SAMPLE_ROWS
    ncols = w.shape[2]
    nbn = ncols // tn
    out, out_s = pl.pallas_call(
        _gelu_proj_kernel,
        grid=(m // tm, nbn),
        in_specs=[pl.BlockSpec((tm, d), lambda i, j: (i, 0)), pl.BlockSpec((ms, d), lambda i, j: (0, 0)),
                  pl.BlockSpec((1, d), lambda i, j: (0, 0)), pl.BlockSpec((1, d, tn), lambda i, j: (layer, 0, j))],
        out_specs=[pl.BlockSpec((tm, tn), lambda i, j: (i, j)),
                   pl.BlockSpec((SAMPLE_ROWS, tn), lambda i, j: (0, jnp.where(i == 0, j, nbn)))],
        out_shape=[jax.ShapeDtypeStruct((m, ncols), bf16), jax.ShapeDtypeStruct((SAMPLE_ROWS, ncols + tn), f32)],
        scratch_shapes=[pltpu.VMEM((tm + SAMPLE_ROWS, d), bf16)],
        compiler_params=_params("arbitrary", "arbitrary"),
        name="gelu_proj",
    )(x, xs, g.reshape(1, d), w)
    return out, out_s[:ms, :ncols]


def _fox_proj_kernel(x_ref, xs_ref, g_ref, w_ref, qg_ref, kg_ref, *refs, tn, nseg, q_scale, qs_scale):
    q_ref, k_ref, k16_ref, v_ref, v16_ref, gate_ref, qs_ref, ks_ref, vs_ref, gs_ref, xn_ref = refs[-11:]
    j = pl.program_id(1)
    tm = x_ref.shape[0]
    ms = xs_ref.shape[0]

    @pl.when(j == 0)
    def _():
        xn_ref[:tm, :] = _rms_rows(x_ref[...], g_ref[...]).astype(bf16)
        xsn = _rms_rows(xs_ref[...], g_ref[...])
        pad = jnp.zeros((SAMPLE_ROWS - ms, xsn.shape[1]), f32)
        xn_ref[tm:, :] = jnp.concatenate([xsn, pad], axis=0).astype(bf16)

    acc = lax.dot_general(xn_ref[...], w_ref[0].astype(bf16), (((1,), (1,)), ((), ())), preferred_element_type=f32)
    seg = j // nseg

    @pl.when(seg == 0)
    def _():
        qn = _head_rms(acc, qg_ref[...], tn)
        q_ref[...] = (qn[:tm] * q_scale).astype(bf16)
        qs_ref[...] = (qn[tm:] * qs_scale).astype(bf16)

    @pl.when(seg == 1)
    def _():
        kn = _head_rms(acc, kg_ref[...], tn)
        k_ref[...] = kn[:tm]
        k16_ref[...] = kn[:tm].astype(bf16)
        ks_ref[...] = kn[tm:]

    @pl.when(seg == 2)
    def _():
        v_ref[...] = acc[:tm]
        v16_ref[...] = acc[:tm].astype(bf16)
        vs_ref[...] = acc[tm:]

    @pl.when(seg == 3)
    def _():
        gate = _sigmoid(acc)
        gate_ref[...] = gate[:tm]
        gs_ref[...] = gate[tm:]


def _fox_proj(x, xs, g, w_t, layer, q_gain, k_gain, k_all, v_all, n_slots, *, tm, q_scale, qs_scale, tn=512):
    m, d = x.shape
    ms = xs.shape[0]
    assert ms <= SAMPLE_ROWS
    nseg = d // tn

    def window(seg):
        return lambda i, j: (i, jnp.clip(j - seg * nseg, 0, nseg - 1))

    def slot_window(seg):
        return lambda i, j: (layer, i, jnp.clip(j - seg * nseg, 0, nseg - 1))

    def sample_window(seg):
        return lambda i, j: (0, jnp.where(i == 0, jnp.clip(j - seg * nseg, 0, nseg - 1), nseg))

    in_specs = [pl.BlockSpec((tm, d), lambda i, j: (i, 0)), pl.BlockSpec((ms, d), lambda i, j: (0, 0)),
                pl.BlockSpec((1, d), lambda i, j: (0, 0)), pl.BlockSpec((1, tn, d), lambda i, j: (layer, j, 0)),
                pl.BlockSpec((1, HEAD_DIM), lambda i, j: (0, 0)), pl.BlockSpec((1, HEAD_DIM), lambda i, j: (0, 0))]
    args = [x, xs, g.reshape(1, d), w_t, q_gain.reshape(1, HEAD_DIM), k_gain.reshape(1, HEAD_DIM)]
    aliases = {}
    for buf, out_idx in ((k_all, 1), (v_all, 3)):
        if buf is not None:
            in_specs.append(pl.BlockSpec(memory_space=pl.ANY))
            args.append(buf)
            aliases[len(args) - 1] = out_idx
    blk = lambda seg: pl.BlockSpec((tm, tn), window(seg))
    slot_blk = lambda seg: pl.BlockSpec((None, tm, tn), slot_window(seg))
    sample_blk = lambda seg: pl.BlockSpec((SAMPLE_ROWS, tn), sample_window(seg))
    full = lambda dt: jax.ShapeDtypeStruct((m, d), dt)
    slots = jax.ShapeDtypeStruct((n_slots, m, d), f32)
    sample = lambda dt: jax.ShapeDtypeStruct((SAMPLE_ROWS, d + tn), dt)
    q, k_all, k16, v_all, v16, gate, q_s, k_s, v_s, gate_s = pl.pallas_call(
        functools.partial(_fox_proj_kernel, tn=tn, nseg=nseg, q_scale=q_scale, qs_scale=qs_scale),
        grid=(m // tm, 4 * nseg),
        in_specs=in_specs,
        out_specs=[blk(0), slot_blk(1), blk(1), slot_blk(2), blk(2), blk(3),
                   sample_blk(0), sample_blk(1), sample_blk(2), sample_blk(3)],
        out_shape=[full(bf16), slots, full(bf16), slots, full(bf16), full(f32),
                   sample(bf16), sample(f32), sample(f32), sample(f32)],
        input_output_aliases=aliases,
        scratch_shapes=[pltpu.VMEM((tm + SAMPLE_ROWS, d), bf16)],
        compiler_params=_params("arbitrary", "arbitrary"),
        name="fox_proj",
    )(*args)
    cut = lambda a: a[:ms, :d]
    return q, k_all, k16, v_all, v16, gate, cut(q_s), cut(k_s), cut(v_s), cut(gate_s)


def _logf_kernel(x_ref, g_ref, w_ref, b_ref, o_ref):
    xn = _rms_rows(x_ref[...], g_ref[...]).astype(bf16)
    z = lax.dot_general(xn, w_ref[...].astype(bf16), (((1,), (1,)), ((), ())),
                        preferred_element_type=f32) + b_ref[...]
    o_ref[...] = jnp.minimum(z, 0.0) - jnp.log1p(jnp.exp(-jnp.abs(z)))


def _logf_proj(x, g, w_ft, b_f, *, tm):
    m, d = x.shape
    return pl.pallas_call(
        _logf_kernel,
        grid=(m // tm,),
        in_specs=[
            pl.BlockSpec((tm, d), lambda i: (i, 0)),
            pl.BlockSpec((1, d), lambda i: (0, 0)),
            pl.BlockSpec((N_HEADS, d), lambda i: (0, 0)),
            pl.BlockSpec((1, N_HEADS), lambda i: (0, 0)),
        ],
        out_specs=pl.BlockSpec((tm, N_HEADS), lambda i: (i, 0)),
        out_shape=jax.ShapeDtypeStruct((m, N_HEADS), f32),
        compiler_params=_params("parallel"),
        name="logf_proj",
    )(x, g.reshape(1, d), w_ft, b_f.reshape(1, N_HEADS))


def _layer_norm_rows(v, g, b):
    mu = jnp.mean(v, axis=-1, keepdims=True)
    vc = v - mu
    var = jnp.mean(vc * vc, axis=-1, keepdims=True)
    return vc * lax.rsqrt(var + LN_EPS) * g + b


def _gmlp_mix_kernel(u_ref, v_ref, g_ref, b_ref, ws_ref, bs_ref, o_ref, *, n_groups, n_chunks):
    vn = _layer_norm_rows(v_ref[...].astype(f32), g_ref[...], b_ref[...]).astype(bf16)
    row = lax.broadcasted_iota(jnp.int32, (CHUNK, CHUNK), 0)
    col = lax.broadcasted_iota(jnp.int32, (CHUNK, CHUNK), 1)
    causal = row >= col
    for gi in range(n_groups):
        cols = slice(gi * GMLP_GROUP, (gi + 1) * GMLP_GROUP)
        ws = jnp.where(causal, ws_ref[gi], 0.0).astype(bf16)
        rhs = jnp.concatenate([vn[c * CHUNK:(c + 1) * CHUNK, cols] for c in range(n_chunks)], axis=1)
        mixed = jnp.dot(ws, rhs, preferred_element_type=f32) + bs_ref[:, gi:gi + 1]
        for c in range(n_chunks):
            rows = slice(c * CHUNK, (c + 1) * CHUNK)
            u = u_ref[rows, cols].astype(f32)
            o_ref[rows, cols] = (u * mixed[:, c * GMLP_GROUP:(c + 1) * GMLP_GROUP]).astype(bf16)


def _gmlp_mix(z, ln_g, ln_b, w_s, b_s_t, *, n_chunks=2):
    m = z.shape[0]
    half = z.shape[1] // 2
    n_groups = half // GMLP_GROUP
    tm = n_chunks * CHUNK
    return pl.pallas_call(
        functools.partial(_gmlp_mix_kernel, n_groups=n_groups, n_chunks=n_chunks),
        grid=(m // tm,),
        in_specs=[
            pl.BlockSpec((tm, half), lambda i: (i, 0)),
            pl.BlockSpec((tm, half), lambda i: (i, 1)),
            pl.BlockSpec((1, half), lambda i: (0, 0)),
            pl.BlockSpec((1, half), lambda i: (0, 0)),
            pl.BlockSpec((n_groups, CHUNK, CHUNK), lambda i: (0, 0, 0)),
            pl.BlockSpec((CHUNK, n_groups), lambda i: (0, 0)),
        ],
        out_specs=pl.BlockSpec((tm, half), lambda i: (i, 0)),
        out_shape=jax.ShapeDtypeStruct((m, half), bf16),
        compiler_params=_params("parallel"),
        name="gmlp_mix",
    )(z, z, ln_g.reshape(1, half), ln_b.reshape(1, half), w_s, b_s_t)


def _gmlp_mix_first_row_kernel(u_ref, v_ref, g_ref, b_ref, w0_ref, b0_ref, o_ref, vn_ref):
    vn = _layer_norm_rows(v_ref[...], g_ref[...], b_ref[...])
    vn_ref[...] = vn
    mixed = vn * w0_ref[...] + b0_ref[...]
    o_ref[...] = (u_ref[...] * mixed).astype(bf16)


def _gmlp_mix_first_row(z, ln_g, ln_b, w00, b0):
    m = z.shape[0]
    half = z.shape[1] // 2
    row = lambda: pl.BlockSpec((1, half), lambda i: (0, 0))
    return pl.pallas_call(
        _gmlp_mix_first_row_kernel,
        grid=(1,),
        in_specs=[
            pl.BlockSpec((m, half), lambda i: (0, 0)),
            pl.BlockSpec((m, half), lambda i: (0, 1)),
            row(), row(), row(), row(),
        ],
        out_specs=[pl.BlockSpec((m, half), lambda i: (0, 0)), pl.BlockSpec((m, half), lambda i: (0, 0))],
        out_shape=[jax.ShapeDtypeStruct((m, half), bf16), jax.ShapeDtypeStruct((m, half), f32)],
        compiler_params=_params("arbitrary"),
        name="gmlp_mix_first_row",
    )(z, z, ln_g.reshape(1, half), ln_b.reshape(1, half), w00.reshape(1, half), b0.reshape(1, half))


def _cumsum_kernel(x_ref, o_ref, *, n_blocks):
    row = lax.broadcasted_iota(jnp.int32, (CHUNK, CHUNK), 0)
    col = lax.broadcasted_iota(jnp.int32, (CHUNK, CHUNK), 1)
    tri = (row >= col).astype(f32)
    carry = jnp.zeros((1, N_HEADS), f32)
    for blk in range(n_blocks):
        rows = slice(blk * CHUNK, (blk + 1) * CHUNK)
        cs = jnp.dot(tri, x_ref[0, rows, :], precision=HIGHEST, preferred_element_type=f32) + carry
        o_ref[0, rows, :] = cs
        carry = cs[CHUNK - 1:CHUNK, :]


def _cumsum_seq(lf):
    b, l, h = lf.shape
    return pl.pallas_call(
        functools.partial(_cumsum_kernel, n_blocks=l // CHUNK),
        grid=(b,),
        in_specs=[pl.BlockSpec((1, l, h), lambda i: (i, 0, 0))],
        out_specs=pl.BlockSpec((1, l, h), lambda i: (i, 0, 0)),
        out_shape=jax.ShapeDtypeStruct((b, l, h), f32),
        compiler_params=_params("parallel"),
        name="cumsum_seq",
    )(lf)


def _split3(x):
    hi = x.astype(bf16).astype(f32)
    r = x - hi
    mid = r.astype(bf16).astype(f32)
    lo = (r - mid).astype(bf16).astype(f32)
    return hi, mid, lo


def _flash_kernel(q_ref, k_ref, v_ref, c_ref, gate_ref, o_ref, kaug_ref, vaug_ref, qside_ref, s_ref, *, tq):
    h = pl.program_id(1)
    qi = pl.program_id(2)
    seq = k_ref.shape[1]

    @pl.when(qi == 0)
    def _():
        head = lax.broadcasted_iota(jnp.int32, (seq, N_HEADS), 1)
        c2 = jnp.sum(jnp.where(head == h, c_ref[0], 0.0), axis=1, keepdims=True) * LOG2E
        hi, mid, lo = _split3(c2)
        lane = lax.broadcasted_iota(jnp.int32, (seq, HEAD_DIM), 1)
        qside = jnp.where(lane == 0, hi, jnp.where(lane == 1, mid, jnp.where(lane == 2, lo,
                          jnp.where(lane < 6, 1.0, 0.0))))
        kside = jnp.where(lane < 3, 1.0, jnp.where(lane == 3, -hi, jnp.where(lane == 4, -mid,
                          jnp.where(lane == 5, -lo, 0.0))))
        qside_ref[...] = qside.astype(bf16)
        kaug_ref[:, :HEAD_DIM] = k_ref[0]
        kaug_ref[:, HEAD_DIM:] = kside.astype(bf16)
        vaug_ref[:, :HEAD_DIM] = v_ref[0]
        vaug_ref[:, HEAD_DIM:] = jnp.where(lane == 0, 1.0, 0.0).astype(bf16)

    q = jnp.concatenate([q_ref[0], qside_ref[pl.ds(pl.multiple_of(qi * tq, tq), tq), :]], axis=1)
    col_minus_row = (lax.broadcasted_iota(jnp.int32, (tq, tq), 1)
                     - lax.broadcasted_iota(jnp.int32, (tq, tq), 0))

    def key_rows(kj):
        return pl.ds(pl.multiple_of(jnp.minimum(kj, qi) * tq, tq), tq)

    def scores(kj):
        return lax.dot_general(q, kaug_ref[key_rows(kj), :], (((1,), (1,)), ((), ())),
                               preferred_element_type=f32)

    def absorb(kj, s, m, acc):
        m_new = jnp.maximum(m, jnp.max(s, axis=1, keepdims=True))
        p = jnp.exp2(s - m_new).astype(bf16)
        pv = jnp.dot(p, vaug_ref[key_rows(kj), :], preferred_element_type=f32)
        return m_new, jnp.exp2(m - m_new) * acc + pv

    s_ref[...] = scores(0)

    def pair(j, carry):
        m, acc = carry
        s_even = s_ref[...]
        s_odd = jnp.where(col_minus_row <= (qi - (2 * j + 1)) * tq, scores(2 * j + 1), -jnp.inf)
        m, acc = absorb(2 * j, s_even, m, acc)
        s_ref[...] = scores(2 * j + 2)
        return absorb(2 * j + 1, s_odd, m, acc)

    init = (jnp.full((tq, 1), -jnp.inf, f32), jnp.zeros((tq, 2 * HEAD_DIM), f32))
    m, acc = lax.fori_loop(0, (qi + 1) // 2, pair, init)

    def finish(acc):
        o = acc[:, :HEAD_DIM] / acc[:, HEAD_DIM:HEAD_DIM + 1]
        o_ref[0] = (o * gate_ref[0]).astype(bf16)

    @pl.when(qi % 2 == 1)
    def _():
        finish(acc)

    @pl.when(qi % 2 == 0)
    def _():
        s = jnp.where(col_minus_row <= 0, s_ref[...], -jnp.inf)
        finish(absorb(qi, s, m, acc)[1])


def _flash_prompt(q, k, v, c, gate, *, tq=512):
    b, l, d = q.shape
    blk = lambda: pl.BlockSpec((1, tq, HEAD_DIM), lambda bi, h, qi: (bi, qi, h))
    seq = lambda: pl.BlockSpec((1, l, HEAD_DIM), lambda bi, h, qi: (bi, 0, h))
    return pl.pallas_call(
        functools.partial(_flash_kernel, tq=tq),
        grid=(b, N_HEADS, l // tq),
        in_specs=[blk(), seq(), seq(), pl.BlockSpec((1, l, N_HEADS), lambda bi, h, qi: (bi, 0, 0)), blk()],
        out_specs=blk(),
        out_shape=jax.ShapeDtypeStruct((b, l, d), bf16),
        scratch_shapes=[pltpu.VMEM((l, 2 * HEAD_DIM), bf16), pltpu.VMEM((l, 2 * HEAD_DIM), bf16),
                        pltpu.VMEM((l, HEAD_DIM), bf16), pltpu.VMEM((tq, tq), f32)],
        compiler_params=_params("parallel", "parallel", "arbitrary"),
        name="flash_prompt",
    )(q, k, v, c, gate)


T_PER_ROW = LANES // N_HEADS
PAGES_PER_STEP = 16


def _eye(n, m):
    return lax.broadcasted_iota(jnp.int32, (n, m), 0) == lax.broadcasted_iota(jnp.int32, (n, m), 1)


def _lane_allreduce(x, op):
    sh = N_HEADS
    while sh < LANES:
        x = op(x, pltpu.roll(x, sh, 1))
        sh *= 2
    return x


def _flat_suffix(x):
    rows = x.shape[0]
    lane = lax.broadcasted_iota(jnp.int32, x.shape, 1)
    row = lax.broadcasted_iota(jnp.int32, x.shape, 0)
    y = x
    sh = N_HEADS
    while sh < LANES:
        y = y + jnp.where(lane < LANES - sh, pltpu.roll(y, LANES - sh, 1), 0.0)
        sh *= 2
    z = _lane_allreduce(jnp.where(lane < N_HEADS, y, 0.0), jnp.add)
    w = z
    sh = 1
    while sh < rows:
        w = w + jnp.where(row < rows - sh, pltpu.roll(w, rows - sh, 0), 0.0)
        sh *= 2
    return (y - x) + (w - z), w[0:1, :]


def _flat_scores(k3, q_rep):
    t = k3.shape[0]
    k2 = k3.reshape(t * N_HEADS, HEAD_DIM).astype(bf16)
    s = lax.dot_general(k2, q_rep, (((1,), (1,)), ((), ())), preferred_element_type=f32)
    s3 = s.reshape(t * N_HEADS // LANES, LANES, LANES)
    return jnp.sum(jnp.where(_eye(LANES, LANES)[None], s3, 0.0), axis=1)


def _dec_scores_kernel(pt_ref, q_ref, *refs, n_steps):
    g_pages = PAGES_PER_STEP
    k_refs, lf_refs = refs[:g_pages], refs[g_pages:2 * g_pages]
    knew_ref, lfnew_ref, s_ref, snew_ref, carry_ref = refs[2 * g_pages:]
    p = pl.program_id(1)

    @pl.when(p == 0)
    def _():
        carry_ref[...] = jnp.zeros_like(carry_ref)

    q_rep = q_ref[0]
    carry = carry_ref[...]
    for g in range(g_pages):
        later, total = _flat_suffix(lf_refs[g][0, 0])
        s_ref[0, g_pages - 1 - g] = _flat_scores(k_refs[g][0, 0], q_rep) + (later + carry + lfnew_ref[0])
        carry = carry + total
    carry_ref[...] = carry

    @pl.when(p == n_steps - 1)
    def _():
        s_new = lax.dot_general(knew_ref[0].astype(bf16), q_rep, (((1,), (1,)), ((), ())),
                                preferred_element_type=f32)
        s_new = jnp.sum(jnp.where(_eye(N_HEADS, LANES), s_new, 0.0), axis=0, keepdims=True)
        row = lax.broadcasted_iota(jnp.int32, snew_ref.shape[1:], 0)
        lane = lax.broadcasted_iota(jnp.int32, snew_ref.shape[1:], 1)
        snew_ref[0] = jnp.where((row == 0) & (lane < N_HEADS), s_new, -jnp.inf)


def _dec_scores(page_table, q_rep, cache_k, lf_flat, layer, k_new, lf_new_rep):
    nb, n_pages = page_table.shape
    page = cache_k.shape[2]
    g_pages = PAGES_PER_STEP
    n_steps = n_pages // g_pages
    rows = page // T_PER_ROW

    def past(g):
        return lambda b, p, pt: pt[b, n_pages - 1 - (p * g_pages + g)]

    k_specs = [pl.BlockSpec((1, 1, page, N_HEADS, HEAD_DIM),
                            lambda b, p, pt, f=past(g): (layer, f(b, p, pt), 0, 0, 0)) for g in range(g_pages)]
    lf_specs = [pl.BlockSpec((1, 1, rows, LANES),
                             lambda b, p, pt, f=past(g): (layer, f(b, p, pt), 0, 0)) for g in range(g_pages)]
    return pl.pallas_call(
        functools.partial(_dec_scores_kernel, n_steps=n_steps),
        grid_spec=pltpu.PrefetchScalarGridSpec(
            num_scalar_prefetch=1,
            grid=(nb, n_steps),
            in_specs=[pl.BlockSpec((1, LANES, HEAD_DIM), lambda b, p, pt: (b, 0, 0))] + k_specs + lf_specs + [
                pl.BlockSpec((1, N_HEADS, HEAD_DIM), lambda b, p, pt: (b, 0, 0)),
                pl.BlockSpec((1, 1, LANES), lambda b, p, pt: (b, 0, 0)),
            ],
            out_specs=[
                pl.BlockSpec((1, g_pages, rows, LANES), lambda b, p, pt: (b, n_steps - 1 - p, 0, 0)),
                pl.BlockSpec((1, SUBLANES, LANES), lambda b, p, pt: (b, 0, 0)),
            ],
            scratch_shapes=[pltpu.VMEM((1, LANES), f32)],
        ),
        out_shape=[jax.ShapeDtypeStruct((nb, n_pages, rows, LANES), f32),
                   jax.ShapeDtypeStruct((nb, SUBLANES, LANES), f32)],
        compiler_params=_params("parallel", "arbitrary"),
        name="dec_scores",
    )(page_table, q_rep, *([cache_k] * g_pages), *([lf_flat] * g_pages), k_new, lf_new_rep)


def _lane_broadcast_rows(flat):
    r = flat.shape[0]
    spread = jnp.where(_eye(LANES, LANES)[None], jnp.broadcast_to(flat[:, None, :], (r, LANES, LANES)), 0.0)
    ones = jnp.ones((LANES, HEAD_DIM), bf16)
    return jnp.dot(spread.reshape(r * LANES, LANES).astype(bf16), ones, preferred_element_type=f32)


def _dec_pv_kernel(pt_ref, s_ref, snew_ref, *refs, n_steps):
    g_pages = PAGES_PER_STEP
    v_refs = refs[:g_pages]
    vnew_ref, gate_ref, o_ref, p_ref, pnew_ref, acc_ref = refs[g_pages:]
    p = pl.program_id(1)

    @pl.when(p == 0)
    def _():
        s = s_ref[0]
        s_new = snew_ref[0]
        m = jnp.maximum(jnp.max(jnp.max(s, axis=0), axis=0, keepdims=True), jnp.max(s_new, axis=0, keepdims=True))
        m = _lane_allreduce(m, jnp.maximum)
        e = jnp.exp(s - m[None])
        e_new = jnp.exp(s_new - m)
        denom = jnp.sum(jnp.sum(e, axis=0), axis=0, keepdims=True) + jnp.sum(e_new, axis=0, keepdims=True)
        denom = _lane_allreduce(denom, jnp.add)
        p_ref[...] = e / denom[None]
        pnew_ref[...] = e_new / denom
        acc_ref[...] = jnp.zeros_like(acc_ref)

    acc = acc_ref[...]
    for g in range(g_pages):
        v3 = v_refs[g][0, 0]
        pb = _lane_broadcast_rows(p_ref[p * g_pages + g])
        acc = acc + jnp.sum(pb.reshape(v3.shape) * v3, axis=0)
    acc_ref[...] = acc

    @pl.when(p == n_steps - 1)
    def _():
        p_new = jnp.where(_eye(N_HEADS, LANES), jnp.broadcast_to(pnew_ref[0:1, :], (N_HEADS, LANES)), 0.0)
        pb = jnp.dot(p_new.astype(bf16), jnp.ones((LANES, HEAD_DIM), bf16), preferred_element_type=f32)
        o_ref[0] = (acc + pb * vnew_ref[0]) * gate_ref[0]


def _dec_pv(page_table, s_all, s_new, cache_v, layer, v_new, gate):
    nb, n_pages = page_table.shape
    page = cache_v.shape[2]
    g_pages = PAGES_PER_STEP
    n_steps = n_pages // g_pages
    rows = page // T_PER_ROW
    head_blk = lambda: pl.BlockSpec((1, N_HEADS, HEAD_DIM), lambda b, p, pt: (b, 0, 0))
    v_specs = [pl.BlockSpec((1, 1, page, N_HEADS, HEAD_DIM),
                            lambda b, p, pt, g=g: (layer, pt[b, p * g_pages + g], 0, 0, 0)) for g in range(g_pages)]
    return pl.pallas_call(
        functools.partial(_dec_pv_kernel, n_steps=n_steps),
        grid_spec=pltpu.PrefetchScalarGridSpec(
            num_scalar_prefetch=1,
            grid=(nb, n_steps),
            in_specs=[
                pl.BlockSpec((1, n_pages, rows, LANES), lambda b, p, pt: (b, 0, 0, 0)),
                pl.BlockSpec((1, SUBLANES, LANES), lambda b, p, pt: (b, 0, 0)),
            ] + v_specs + [head_blk(), head_blk()],
            out_specs=head_blk(),
            scratch_shapes=[pltpu.VMEM((n_pages, rows, LANES), f32), pltpu.VMEM((SUBLANES, LANES), f32),
                            pltpu.VMEM((N_HEADS, HEAD_DIM), f32)],
        ),
        out_shape=jax.ShapeDtypeStruct((nb, N_HEADS, HEAD_DIM), f32),
        compiler_params=_params("parallel", "arbitrary"),
        name="dec_pv",
    )(page_table, s_all, s_new, *([cache_v] * g_pages), v_new, gate)


def _final_norm_kernel(x_ref, g_ref, o_ref):
    o_ref[...] = _rms_rows(x_ref[...], g_ref[...])


def _final_norm(x, g, *, tm):
    m, d = x.shape
    return pl.pallas_call(
        _final_norm_kernel,
        grid=(m // tm,),
        in_specs=[pl.BlockSpec((tm, d), lambda i: (i, 0)), pl.BlockSpec((1, d), lambda i: (0, 0))],
        out_specs=pl.BlockSpec((tm, d), lambda i: (i, 0)),
        out_shape=jax.ShapeDtypeStruct((m, d), f32),
        compiler_params=_params("parallel"),
        name="final_norm",
    )(x, g.reshape(1, d))


def kernel(x_prompt, x_sample, cache_k, cache_v, cache_logf, page_table, norm_ffn1, ffn1_w_up, ffn1_w_down, norm_mix, norm_ffn2, ffn2_w_up, ffn2_w_down, gm_w_in, gm_ln_g, gm_ln_b, gm_w_s, gm_b_s, gm_w_out, fox_w_in, fox_b_f, fox_q_norm, fox_k_norm, fox_w_out, norm_final):
    bp, lp, d = x_prompt.shape
    bs, ls, _ = x_sample.shape
    assert ls == 1 and lp % 512 == 0 and d == N_HEADS * HEAD_DIM
    depth = norm_ffn1.shape[0]
    mp, ms = bp * lp, bs * ls
    tm_p = min(1024, mp)
    half = gm_w_in.shape[2] // 2

    xp = x_prompt.reshape(mp, d)
    xs = x_sample.reshape(ms, d)
    lf_p, k_s, v_s, lf_s, gv_s = [], [], [], [], []
    kp_all = vp_all = None
    n_fox = fox_w_in.shape[0]
    n_pool, page = cache_logf.shape[1:3]
    assert page == LANES and page_table.shape[1] % PAGES_PER_STEP == 0
    fox_w_t = jnp.swapaxes(fox_w_in, 1, 2)
    lf_flat = cache_logf.reshape(n_fox, n_pool, page // T_PER_ROW, LANES)

    for i in range(depth):
        xp, xs = _ffn(xp, xs, norm_ffn1[i], ffn1_w_up, ffn1_w_down, i, tm_up=tm_p, tm_down=512)
        j = i // 2
        if i % 2 == 0:
            zp, zs = _gelu_proj(xp, xs, norm_mix[i], gm_w_in, j, tm=tm_p)
            gp = _gmlp_mix(zp, gm_ln_g[j], gm_ln_b[j], gm_w_s[j], gm_b_s[j].T)
            w00 = jnp.repeat(gm_w_s[j, :, 0, 0], GMLP_GROUP)
            b0 = jnp.repeat(gm_b_s[j, :, 0], GMLP_GROUP)
            gs, gv = _gmlp_mix_first_row(zs, gm_ln_g[j], gm_ln_b[j], w00, b0)
            gv_s.append(gv.reshape(bs, ls, half))
            xp, xs = _mm_res(gp, gs, gm_w_out, j, xp, xs, 1.0, tm=512)
        else:
            w_ft = fox_w_t[j, 4 * d:, :]
            qp, kp_all, kp16, vp_all, vp16, gatep, qs, ks, vs, gates = _fox_proj(
                xp, xs, norm_mix[i], fox_w_t, j, fox_q_norm[j], fox_k_norm[j], kp_all, vp_all, n_fox,
                tm=tm_p, q_scale=SCORE_SCALE * LOG2E, qs_scale=SCORE_SCALE)
            lfp = _logf_proj(xp, norm_mix[i], w_ft, fox_b_f[j], tm=tm_p).reshape(bp, lp, N_HEADS)
            og = _flash_prompt(qp.reshape(bp, lp, d), kp16.reshape(bp, lp, d), vp16.reshape(bp, lp, d),
                               _cumsum_seq(lfp), gatep.reshape(bp, lp, d))
            lf_p.append(lfp)
            lfs = _logf_proj(xs, norm_mix[i], w_ft, fox_b_f[j], tm=ms)
            q_rep = jnp.tile(qs.reshape(bs, N_HEADS, HEAD_DIM), (1, T_PER_ROW, 1))
            lf_new_rep = jnp.tile(lfs, (1, T_PER_ROW)).reshape(bs, 1, LANES)
            ks3 = ks.reshape(bs, N_HEADS, HEAD_DIM)
            vs3 = vs.reshape(bs, N_HEADS, HEAD_DIM)
            k_s.append(ks3.reshape(bs, ls, N_HEADS, HEAD_DIM))
            v_s.append(vs3.reshape(bs, ls, N_HEADS, HEAD_DIM))
            s_all, s_new = _dec_scores(page_table, q_rep, cache_k, lf_flat, j, ks3, lf_new_rep)
            os_ = _dec_pv(page_table, s_all, s_new, cache_v, j, vs3, gates.reshape(bs, N_HEADS, HEAD_DIM))
            lf_s.append(lfs.reshape(bs, ls, N_HEADS))
            xp, xs = _mm_res(og.reshape(mp, d), os_.reshape(ms, d).astype(bf16), fox_w_out, j, xp, xs, 1.0, tm=512)
        xp, xs = _ffn(xp, xs, norm_ffn2[i], ffn2_w_up, ffn2_w_down, i, tm_up=tm_p, tm_down=512)

    y_prompt = _final_norm(xp, norm_final, tm=512).reshape(bp, lp, d)
    y_sample = _final_norm(xs, norm_final, tm=ms).reshape(bs, ls, d)
    heads_p = (n_fox, bp, lp, N_HEADS, HEAD_DIM)
    return (y_prompt, y_sample, kp_all.reshape(heads_p), vp_all.reshape(heads_p), jnp.stack(lf_p),
            jnp.stack(k_s), jnp.stack(v_s), jnp.stack(lf_s), jnp.stack(gv_s))
```

```python
import functools

import jax
import jax.numpy as jnp
from jax import lax
from jax.experimental import pallas as pl
from jax.experimental.pallas import tpu as pltpu

f32 = jnp.float32
bf16 = jnp.bfloat16

N_HEADS = 16
HEAD_DIM = 128
CHUNK = 128
GMLP_GROUP = 128
NORM_EPS = 1e-6
LN_EPS = 1e-5
FFN_RESIDUAL = 0.5
LANES = 128
SUBLANES = 8
MXU_COLS = 256
ROW_TILE = 1024
FLASH_TILE = 512

VMEM_LIMIT_BYTES = 56 * 1024 * 1024
HIGHEST = lax.Precision.HIGHEST
LOG2E = 1.4426950408889634
SCORE_SCALE = HEAD_DIM ** -0.5


def _params(*sem):
    return pltpu.CompilerParams(dimension_semantics=sem, vmem_limit_bytes=VMEM_LIMIT_BYTES)


def _rms_rows(x, g):
    ms = jnp.mean(x * x, axis=-1, keepdims=True)
    return x * lax.rsqrt(ms + NORM_EPS) * g


def _sigmoid(x):
    return 1.0 / (1.0 + jnp.exp(-x))


SAMPLE_ROWS = 16


def _ffn_up_kernel(x_ref, xs_ref, g_ref, wa_ref, wb_ref, o_ref, os_ref, xn_ref, *, nb, tn, nvalid, sub):
    j = pl.program_id(1)
    tm = x_ref.shape[0]
    ms = xs_ref.shape[0]

    @pl.when(j == 0)
    def _():
        xn_ref[:tm, :] = _rms_rows(x_ref[...], g_ref[...]).astype(bf16)
        xsn = _rms_rows(xs_ref[...], g_ref[...])
        pad = jnp.zeros((SAMPLE_ROWS - ms, xsn.shape[1]), f32)
        xn_ref[tm:, :] = jnp.concatenate([xsn, pad], axis=0).astype(bf16)

    def emit(ncols, b_off):
        xn = xn_ref[...]
        for c0 in range(0, ncols, sub):
            c1 = min(c0 + sub, ncols)
            wa = wa_ref[0, :, c0:c1].astype(bf16)
            wb = wb_ref[0, :, b_off + c0:b_off + c1].astype(bf16)
            if 2 * (c1 - c0) <= MXU_COLS:
                ab = jnp.dot(xn, jnp.concatenate([wa, wb], axis=1), preferred_element_type=f32)
                a, b = ab[:, :c1 - c0], ab[:, c1 - c0:]
            else:
                a = jnp.dot(xn, wa, preferred_element_type=f32)
                b = jnp.dot(xn, wb, preferred_element_type=f32)
            h = (a * _sigmoid(a) * b).astype(o_ref.dtype)
            o_ref[:, c0:c1] = h[:tm]
            os_ref[:, c0:c1] = h[tm:]
        if ncols < tn:
            os_ref[:, ncols:] = jnp.zeros((SAMPLE_ROWS, tn - ncols), os_ref.dtype)

    if nvalid == tn:
        emit(tn, 0)
    else:
        @pl.when(j < nb - 1)
        def _():
            emit(tn, 0)

        @pl.when(j == nb - 1)
        def _():
            emit(nvalid, tn - nvalid)


def _ffn_up(x, xs, g, w_up, layer, *, tm, tn=512, sub=256):
    m, d = x.shape
    ms = xs.shape[0]
    assert ms <= SAMPLE_ROWS
    dff = w_up.shape[2] // 2
    nb = pl.cdiv(dff, tn)
    nvalid = dff - (nb - 1) * tn
    kern = functools.partial(_ffn_up_kernel, nb=nb, tn=tn, nvalid=nvalid, sub=sub)
    h, hs = pl.pallas_call(
        kern,
        grid=(m // tm, nb),
        in_specs=[
            pl.BlockSpec((tm, d), lambda i, j: (i, 0)),
            pl.BlockSpec((ms, d), lambda i, j: (0, 0)),
            pl.BlockSpec((1, d), lambda i, j: (0, 0)),
            pl.BlockSpec((1, d, tn), lambda i, j: (layer, 0, j)),
            pl.BlockSpec((pl.Element(1), pl.Element(d), pl.Element(tn)),
                         lambda i, j: (layer, 0, pl.multiple_of(jnp.minimum(dff + j * tn, 2 * dff - tn), LANES))),
        ],
        out_specs=[pl.BlockSpec((tm, tn), lambda i, j: (i, j)),
                   pl.BlockSpec((SAMPLE_ROWS, tn), lambda i, j: (0, jnp.where(i == 0, j, nb)))],
        out_shape=[jax.ShapeDtypeStruct((m, dff), bf16),
                   jax.ShapeDtypeStruct((SAMPLE_ROWS, (nb + 1) * tn), bf16)],
        scratch_shapes=[pltpu.VMEM((tm + SAMPLE_ROWS, d), bf16)],
        compiler_params=_params("arbitrary", "arbitrary"),
        name="ffn_up",
    )(x, xs, g.reshape(1, d), w_up, w_up)
    return h, hs[:ms, :dff]


def _mm_res_kernel(h_ref, hs_ref, w_ref, r_ref, rs_ref, o_ref, os_ref, wb_ref, *, scale):
    @pl.when(pl.program_id(1) == 0)
    def _():
        wb_ref[...] = w_ref[0].astype(bf16)
        os_ref[...] = rs_ref[...] + scale * jnp.dot(hs_ref[...], wb_ref[...], preferred_element_type=f32)

    acc = jnp.dot(h_ref[...], wb_ref[...], preferred_element_type=f32)
    o_ref[...] = r_ref[...] + scale * acc


def _mm_res(h, hs, w, layer, res, res_s, scale, *, tm, tn=512):
    m, k = h.shape
    ms = hs.shape[0]
    n = w.shape[2]
    return pl.pallas_call(
        functools.partial(_mm_res_kernel, scale=scale),
        grid=(n // tn, m // tm),
        in_specs=[
            pl.BlockSpec((tm, k), lambda j, i: (i, 0)),
            pl.BlockSpec((ms, k), lambda j, i: (0, 0)),
            pl.BlockSpec((1, k, tn), lambda j, i: (layer, 0, j)),
            pl.BlockSpec((tm, tn), lambda j, i: (i, j)),
            pl.BlockSpec((ms, tn), lambda j, i: (0, j)),
        ],
        out_specs=[pl.BlockSpec((tm, tn), lambda j, i: (i, j)), pl.BlockSpec((ms, tn), lambda j, i: (0, j))],
        out_shape=[jax.ShapeDtypeStruct((m, n), f32), jax.ShapeDtypeStruct((ms, n), f32)],
        scratch_shapes=[pltpu.VMEM((k, tn), bf16)],
        compiler_params=_params("parallel", "arbitrary"),
        name="mm_res",
    )(h, hs, w, res, res_s)


def _ffn(x, xs, g, w_up, w_down, layer, *, tm_up, tm_down):
    h, hs = _ffn_up(x, xs, g, w_up, layer, tm=tm_up)
    return _mm_res(h, hs, w_down, layer, x, xs, FFN_RESIDUAL, tm=tm_down)


def _head_rms(acc, gain, tn):
    outs = []
    for hh in range(tn // HEAD_DIM):
        blk = acc[:, hh * HEAD_DIM:(hh + 1) * HEAD_DIM]
        ms = jnp.mean(blk * blk, axis=-1, keepdims=True)
        outs.append(blk * lax.rsqrt(ms + NORM_EPS) * gain)
    return jnp.concatenate(outs, axis=1)


def _gelu_proj_kernel(x_ref, xs_ref, g_ref, w_ref, o_ref, os_ref, xn_ref):
    tm = x_ref.shape[0]
    ms = xs_ref.shape[0]

    @pl.when(pl.program_id(1) == 0)
    def _():
        xn_ref[:tm, :] = _rms_rows(x_ref[...], g_ref[...]).astype(bf16)
        xsn = _rms_rows(xs_ref[...], g_ref[...])
        pad = jnp.zeros((SAMPLE_ROWS - ms, xsn.shape[1]), f32)
        xn_ref[tm:, :] = jnp.concatenate([xsn, pad], axis=0).astype(bf16)

    acc = jnp.dot(xn_ref[...], w_ref[0].astype(bf16), preferred_element_type=f32)
    z = 0.5 * acc * (1.0 + lax.erf(acc * (0.5 ** 0.5)))
    o_ref[...] = z[:tm].astype(o_ref.dtype)
    os_ref[...] = z[tm:]


def _gelu_proj(x, xs, g, w, layer, *, tm, tn=512):
    m, d = x.shape
    ms = xs.shape[0]
    assert ms <= SAMPLE_ROWS
    ncols = w.shape[2]
    nbn = ncols // tn
    out, out_s = pl.pallas_call(
        _gelu_proj_kernel,
        grid=(m // tm, nbn),
        in_specs=[pl.BlockSpec((tm, d), lambda i, j: (i, 0)), pl.BlockSpec((ms, d), lambda i, j: (0, 0)),
                  pl.BlockSpec((1, d), lambda i, j: (0, 0)), pl.BlockSpec((1, d, tn), lambda i, j: (layer, 0, j))],
        out_specs=[pl.BlockSpec((tm, tn), lambda i, j: (i, j)),
                   pl.BlockSpec((SAMPLE_ROWS, tn), lambda i, j: (0, jnp.where(i == 0, j, nbn)))],
        out_shape=[jax.ShapeDtypeStruct((m, ncols), bf16), jax.ShapeDtypeStruct((SAMPLE_ROWS, ncols + tn), f32)],
        scratch_shapes=[pltpu.VMEM((tm + SAMPLE_ROWS, d), bf16)],
        compiler_params=_params("arbitrary", "arbitrary"),
        name="gelu_proj",
    )(x, xs, g.reshape(1, d), w)
    return out, out_s[:ms, :ncols]


def _fox_proj_kernel(x_ref, xs_ref, g_ref, w_ref, qg_ref, kg_ref, *refs, tn, nseg, q_scale, qs_scale):
    q_ref, k_ref, k16_ref, v_ref, v16_ref, gate_ref, qs_ref, ks_ref, vs_ref, gs_ref, xn_ref = refs[-11:]
    j = pl.program_id(1)
    tm = x_ref.shape[0]
    ms = xs_ref.shape[0]

    @pl.when(j == 0)
    def _():
        xn_ref[:tm, :] = _rms_rows(x_ref[...], g_ref[...]).astype(bf16)
        xsn = _rms_rows(xs_ref[...], g_ref[...])
        pad = jnp.zeros((SAMPLE_ROWS - ms, xsn.shape[1]), f32)
        xn_ref[tm:, :] = jnp.concatenate([xsn, pad], axis=0).astype(bf16)

    acc = lax.dot_general(xn_ref[...], w_ref[0].astype(bf16), (((1,), (1,)), ((), ())), preferred_element_type=f32)
    seg = j // nseg

    @pl.when(seg == 0)
    def _():
        qn = _head_rms(acc, qg_ref[...], tn)
        q_ref[...] = (qn[:tm] * q_scale).astype(bf16)
        qs_ref[...] = (qn[tm:] * qs_scale).astype(bf16)

    @pl.when(seg == 1)
    def _():
        kn = _head_rms(acc, kg_ref[...], tn)
        k_ref[...] = kn[:tm]
        k16_ref[...] = kn[:tm].astype(bf16)
        ks_ref[...] = kn[tm:]

    @pl.when(seg == 2)
    def _():
        v_ref[...] = acc[:tm]
        v16_ref[...] = acc[:tm].astype(bf16)
        vs_ref[...] = acc[tm:]

    @pl.when(seg == 3)
    def _():
        gate = _sigmoid(acc)
        gate_ref[...] = gate[:tm]
        gs_ref[...] = gate[tm:]


def _fox_proj(x, xs, g, w_t, layer, q_gain, k_gain, k_all, v_all, n_slots, *, tm, q_scale, qs_scale, tn=512):
    m, d = x.shape
    ms = xs.shape[0]
    assert ms <= SAMPLE_ROWS
    nseg = d // tn

    def window(seg):
        return lambda i, j: (i, jnp.clip(j - seg * nseg, 0, nseg - 1))

    def slot_window(seg):
        return lambda i, j: (layer, i, jnp.clip(j - seg * nseg, 0, nseg - 1))

    def sample_window(seg):
        return lambda i, j: (0, jnp.where(i == 0, jnp.clip(j - seg * nseg, 0, nseg - 1), nseg))

    in_specs = [pl.BlockSpec((tm, d), lambda i, j: (i, 0)), pl.BlockSpec((ms, d), lambda i, j: (0, 0)),
                pl.BlockSpec((1, d), lambda i, j: (0, 0)), pl.BlockSpec((1, tn, d), lambda i, j: (layer, j, 0)),
                pl.BlockSpec((1, HEAD_DIM), lambda i, j: (0, 0)), pl.BlockSpec((1, HEAD_DIM), lambda i, j: (0, 0))]
    args = [x, xs, g.reshape(1, d), w_t, q_gain.reshape(1, HEAD_DIM), k_gain.reshape(1, HEAD_DIM)]
    aliases = {}
    for buf, out_idx in ((k_all, 1), (v_all, 3)):
        if buf is not None:
            in_specs.append(pl.BlockSpec(memory_space=pl.ANY))
            args.append(buf)
            aliases[len(args) - 1] = out_idx
    blk = lambda seg: pl.BlockSpec((tm, tn), window(seg))
    slot_blk = lambda seg: pl.BlockSpec((None, tm, tn), slot_window(seg))
    sample_blk = lambda seg: pl.BlockSpec((SAMPLE_ROWS, tn), sample_window(seg))
    full = lambda dt: jax.ShapeDtypeStruct((m, d), dt)
    slots = jax.ShapeDtypeStruct((n_slots, m, d), f32)
    sample = lambda dt: jax.ShapeDtypeStruct((SAMPLE_ROWS, d + tn), dt)
    q, k_all, k16, v_all, v16, gate, q_s, k_s, v_s, gate_s = pl.pallas_call(
        functools.partial(_fox_proj_kernel, tn=tn, nseg=nseg, q_scale=q_scale, qs_scale=qs_scale),
        grid=(m // tm, 4 * nseg),
        in_specs=in_specs,
        out_specs=[blk(0), slot_blk(1), blk(1), slot_blk(2), blk(2), blk(3),
                   sample_blk(0), sample_blk(1), sample_blk(2), sample_blk(3)],
        out_shape=[full(bf16), slots, full(bf16), slots, full(bf16), full(f32),
                   sample(bf16), sample(f32), sample(f32), sample(f32)],
        input_output_aliases=aliases,
        scratch_shapes=[pltpu.VMEM((tm + SAMPLE_ROWS, d), bf16)],
        compiler_params=_params("arbitrary", "arbitrary"),
        name="fox_proj",
    )(*args)
    cut = lambda a: a[:ms, :d]
    return q, k_all, k16, v_all, v16, gate, cut(q_s), cut(k_s), cut(v_s), cut(gate_s)


def _logf_kernel(x_ref, g_ref, w_ref, b_ref, o_ref):
    xn = _rms_rows(x_ref[...], g_ref[...]).astype(bf16)
    z = lax.dot_general(xn, w_ref[...].astype(bf16), (((1,), (1,)), ((), ())),
                        preferred_element_type=f32) + b_ref[...]
    o_ref[...] = jnp.minimum(z, 0.0) - jnp.log1p(jnp.exp(-jnp.abs(z)))


def _logf_proj(x, g, w_ft, b_f, *, tm):
    m, d = x.shape
    return pl.pallas_call(
        _logf_kernel,
        grid=(m // tm,),
        in_specs=[
            pl.BlockSpec((tm, d), lambda i: (i, 0)),
            pl.BlockSpec((1, d), lambda i: (0, 0)),
            pl.BlockSpec((N_HEADS, d), lambda i: (0, 0)),
            pl.BlockSpec((1, N_HEADS), lambda i: (0, 0)),
        ],
        out_specs=pl.BlockSpec((tm, N_HEADS), lambda i: (i, 0)),
        out_shape=jax.ShapeDtypeStruct((m, N_HEADS), f32),
        compiler_params=_params("parallel"),
        name="logf_proj",
    )(x, g.reshape(1, d), w_ft, b_f.reshape(1, N_HEADS))


def _layer_norm_rows(v, g, b):
    mu = jnp.mean(v, axis=-1, keepdims=True)
    vc = v - mu
    var = jnp.mean(vc * vc, axis=-1, keepdims=True)
    return vc * lax.rsqrt(var + LN_EPS) * g + b


def _gmlp_mix_kernel(u_ref, v_ref, g_ref, b_ref, ws_ref, bs_ref, o_ref, *, n_groups, n_chunks):
    vn = _layer_norm_rows(v_ref[...].astype(f32), g_ref[...], b_ref[...]).astype(bf16)
    row = lax.broadcasted_iota(jnp.int32, (CHUNK, CHUNK), 0)
    col = lax.broadcasted_iota(jnp.int32, (CHUNK, CHUNK), 1)
    causal = row >= col
    for gi in range(n_groups):
        cols = slice(gi * GMLP_GROUP, (gi + 1) * GMLP_GROUP)
        ws = jnp.where(causal, ws_ref[gi], 0.0).astype(bf16)
        rhs = jnp.concatenate([vn[c * CHUNK:(c + 1) * CHUNK, cols] for c in range(n_chunks)], axis=1)
        mixed = jnp.dot(ws, rhs, preferred_element_type=f32) + bs_ref[:, gi:gi + 1]
        for c in range(n_chunks):
            rows = slice(c * CHUNK, (c + 1) * CHUNK)
            u = u_ref[rows, cols].astype(f32)
            o_ref[rows, cols] = (u * mixed[:, c * GMLP_GROUP:(c + 1) * GMLP_GROUP]).astype(bf16)


def _gmlp_mix(z, ln_g, ln_b, w_s, b_s_t, *, n_chunks=2):
    m = z.shape[0]
    half = z.shape[1] // 2
    n_groups = half // GMLP_GROUP
    tm = n_chunks * CHUNK
    return pl.pallas_call(
        functools.partial(_gmlp_mix_kernel, n_groups=n_groups, n_chunks=n_chunks),
        grid=(m // tm,),
        in_specs=[
            pl.BlockSpec((tm, half), lambda i: (i, 0)),
            pl.BlockSpec((tm, half), lambda i: (i, 1)),
            pl.BlockSpec((1, half), lambda i: (0, 0)),
            pl.BlockSpec((1, half), lambda i: (0, 0)),
            pl.BlockSpec((n_groups, CHUNK, CHUNK), lambda i: (0, 0, 0)),
            pl.BlockSpec((CHUNK, n_groups), lambda i: (0, 0)),
        ],
        out_specs=pl.BlockSpec((tm, half), lambda i: (i, 0)),
        out_shape=jax.ShapeDtypeStruct((m, half), bf16),
        compiler_params=_params("parallel"),
        name="gmlp_mix",
    )(z, z, ln_g.reshape(1, half), ln_b.reshape(1, half), w_s, b_s_t)


def _gmlp_mix_first_row_kernel(u_ref, v_ref, g_ref, b_ref, w0_ref, b0_ref, o_ref, vn_ref):
    vn = _layer_norm_rows(v_ref[...], g_ref[...], b_ref[...])
    vn_ref[...] = vn
    mixed = vn * w0_ref[...] + b0_ref[...]
    o_ref[...] = (u_ref[...] * mixed).astype(bf16)


def _gmlp_mix_first_row(z, ln_g, ln_b, w00, b0):
    m = z.shape[0]
    half = z.shape[1] // 2
    row = lambda: pl.BlockSpec((1, half), lambda i: (0, 0))
    return pl.pallas_call(
        _gmlp_mix_first_row_kernel,
        grid=(1,),
        in_specs=[
            pl.BlockSpec((m, half), lambda i: (0, 0)),
            pl.BlockSpec((m, half), lambda i: (0, 1)),
            row(), row(), row(), row(),
        ],
        out_specs=[pl.BlockSpec((m, half), lambda i: (0, 0)), pl.BlockSpec((m, half), lambda i: (0, 0))],
        out_shape=[jax.ShapeDtypeStruct((m, half), bf16), jax.ShapeDtypeStruct((m, half), f32)],
        compiler_params=_params("arbitrary"),
        name="gmlp_mix_first_row",
    )(z, z, ln_g.reshape(1, half), ln_b.reshape(1, half), w00.reshape(1, half), b0.reshape(1, half))


def _cumsum_kernel(x_ref, o_ref, *, n_blocks):
    row = lax.broadcasted_iota(jnp.int32, (CHUNK, CHUNK), 0)
    col = lax.broadcasted_iota(jnp.int32, (CHUNK, CHUNK), 1)
    tri = (row >= col).astype(f32)
    carry = jnp.zeros((1, N_HEADS), f32)
    for blk in range(n_blocks):
        rows = slice(blk * CHUNK, (blk + 1) * CHUNK)
        cs = jnp.dot(tri, x_ref[0, rows, :], precision=HIGHEST, preferred_element_type=f32) + carry
        o_ref[0, rows, :] = cs
        carry = cs[CHUNK - 1:CHUNK, :]


def _cumsum_seq(lf):
    b, l, h = lf.shape
    return pl.pallas_call(
        functools.partial(_cumsum_kernel, n_blocks=l // CHUNK),
        grid=(b,),
        in_specs=[pl.BlockSpec((1, l, h), lambda i: (i, 0, 0))],
        out_specs=pl.BlockSpec((1, l, h), lambda i: (i, 0, 0)),
        out_shape=jax.ShapeDtypeStruct((b, l, h), f32),
        compiler_params=_params("parallel"),
        name="cumsum_seq",
    )(lf)


def _split3(x):
    hi = x.astype(bf16).astype(f32)
    r = x - hi
    mid = r.astype(bf16).astype(f32)
    lo = (r - mid).astype(bf16).astype(f32)
    return hi, mid, lo


def _flash_kernel(q_ref, k_ref, v_ref, c_ref, gate_ref, o_ref, kaug_ref, vaug_ref, qside_ref, s_ref, *, tq):
    h = pl.program_id(1)
    qi = pl.program_id(2)
    seq = k_ref.shape[1]

    @pl.when(qi == 0)
    def _():
        head = lax.broadcasted_iota(jnp.int32, (seq, N_HEADS), 1)
        c2 = jnp.sum(jnp.where(head == h, c_ref[0], 0.0), axis=1, keepdims=True) * LOG2E
        hi, mid, lo = _split3(c2)
        lane = lax.broadcasted_iota(jnp.int32, (seq, HEAD_DIM), 1)
        qside = jnp.where(lane == 0, hi, jnp.where(lane == 1, mid, jnp.where(lane == 2, lo,
                          jnp.where(lane < 6, 1.0, 0.0))))
        kside = jnp.where(lane < 3, 1.0, jnp.where(lane == 3, -hi, jnp.where(lane == 4, -mid,
                          jnp.where(lane == 5, -lo, 0.0))))
        qside_ref[...] = qside.astype(bf16)
        kaug_ref[:, :HEAD_DIM] = k_ref[0]
        kaug_ref[:, HEAD_DIM:] = kside.astype(bf16)
        vaug_ref[:, :HEAD_DIM] = v_ref[0]
        vaug_ref[:, HEAD_DIM:] = jnp.where(lane == 0, 1.0, 0.0).astype(bf16)

    q = jnp.concatenate([q_ref[0], qside_ref[pl.ds(pl.multiple_of(qi * tq, tq), tq), :]], axis=1)
    col_minus_row = (lax.broadcasted_iota(jnp.int32, (tq, tq), 1)
                     - lax.broadcasted_iota(jnp.int32, (tq, tq), 0))

    def key_rows(kj):
        return pl.ds(pl.multiple_of(jnp.minimum(kj, qi) * tq, tq), tq)

    def scores(kj):
        return lax.dot_general(q, kaug_ref[key_rows(kj), :], (((1,), (1,)), ((), ())),
                               preferred_element_type=f32)

    def absorb(kj, s, m, acc):
        m_new = jnp.maximum(m, jnp.max(s, axis=1, keepdims=True))
        p = jnp.exp2(s - m_new).astype(bf16)
        pv = jnp.dot(p, vaug_ref[key_rows(kj), :], preferred_element_type=f32)
        return m_new, jnp.exp2(m - m_new) * acc + pv

    s_ref[...] = scores(0)

    def pair(j, carry):
        m, acc = carry
        s_even = s_ref[...]
        s_odd = jnp.where(col_minus_row <= (qi - (2 * j + 1)) * tq, scores(2 * j + 1), -jnp.inf)
        m, acc = absorb(2 * j, s_even, m, acc)
        s_ref[...] = scores(2 * j + 2)
        return absorb(2 * j + 1, s_odd, m, acc)

    init = (jnp.full((tq, 1), -jnp.inf, f32), jnp.zeros((tq, 2 * HEAD_DIM), f32))
    m, acc = lax.fori_loop(0, (qi + 1) // 2, pair, init)

    def finish(acc):
        o = acc[:, :HEAD_DIM] / acc[:, HEAD_DIM:HEAD_DIM + 1]
        o_ref[0] = (o * gate_ref[0]).astype(bf16)

    @pl.when(qi % 2 == 1)
    def _():
        finish(acc)

    @pl.when(qi % 2 == 0)
    def _():
        s = jnp.where(col_minus_row <= 0, s_ref[...], -jnp.inf)
        finish(absorb(qi, s, m, acc)[1])


def _flash_prompt(q, k, v, c, gate, *, tq=FLASH_TILE):
    b, l, d = q.shape
    blk = lambda: pl.BlockSpec((1, tq, HEAD_DIM), lambda bi, h, qi: (bi, qi, h))
    seq = lambda: pl.BlockSpec((1, l, HEAD_DIM), lambda bi, h, qi: (bi, 0, h))
    return pl.pallas_call(
        functools.partial(_flash_kernel, tq=tq),
        grid=(b, N_HEADS, l // tq),
        in_specs=[blk(), seq(), seq(), pl.BlockSpec((1, l, N_HEADS), lambda bi, h, qi: (bi, 0, 0)), blk()],
        out_specs=blk(),
        out_shape=jax.ShapeDtypeStruct((b, l, d), bf16),
        scratch_shapes=[pltpu.VMEM((l, 2 * HEAD_DIM), bf16), pltpu.VMEM((l, 2 * HEAD_DIM), bf16),
                        pltpu.VMEM((l, HEAD_DIM), bf16), pltpu.VMEM((tq, tq), f32)],
        compiler_params=_params("parallel", "parallel", "arbitrary"),
        name="flash_prompt",
    )(q, k, v, c, gate)


T_PER_ROW = LANES // N_HEADS
SCORE_PAGES_PER_STEP = 8
PV_PAGES_PER_STEP = 16


def _eye(n, m):
    return lax.broadcasted_iota(jnp.int32, (n, m), 0) == lax.broadcasted_iota(jnp.int32, (n, m), 1)


def _lane_allreduce(x, op):
    sh = N_HEADS
    while sh < LANES:
        x = op(x, pltpu.roll(x, sh, 1))
        sh *= 2
    return x


def _flat_suffix(x):
    rows = x.shape[0]
    lane = lax.broadcasted_iota(jnp.int32, x.shape, 1)
    row = lax.broadcasted_iota(jnp.int32, x.shape, 0)
    y = x
    sh = N_HEADS
    while sh < LANES:
        y = y + jnp.where(lane < LANES - sh, pltpu.roll(y, LANES - sh, 1), 0.0)
        sh *= 2
    z = _lane_allreduce(jnp.where(lane < N_HEADS, y, 0.0), jnp.add)
    w = z
    sh = 1
    while sh < rows:
        w = w + jnp.where(row < rows - sh, pltpu.roll(w, rows - sh, 0), 0.0)
        sh *= 2
    return (y - x) + (w - z), w[0:1, :]


def _flat_scores(k3, q_rep):
    t = k3.shape[0]
    k2 = k3.reshape(t * N_HEADS, HEAD_DIM).astype(bf16)
    s = lax.dot_general(k2, q_rep, (((1,), (1,)), ((), ())), preferred_element_type=f32)
    s3 = s.reshape(t * N_HEADS // LANES, LANES, LANES)
    return jnp.sum(jnp.where(_eye(LANES, LANES)[None], s3, 0.0), axis=1)


def _dec_scores_kernel(pt_ref, q_ref, *refs, n_steps):
    g_pages = SCORE_PAGES_PER_STEP
    k_refs, lf_refs = refs[:g_pages], refs[g_pages:2 * g_pages]
    knew_ref, lfnew_ref, s_ref, snew_ref, carry_ref = refs[2 * g_pages:]
    p = pl.program_id(1)

    @pl.when(p == 0)
    def _():
        carry_ref[...] = jnp.zeros_like(carry_ref)

    q_rep = q_ref[0]
    carry = carry_ref[...]
    for g in range(g_pages):
        later, total = _flat_suffix(lf_refs[g][0, 0])
        s_ref[0, g_pages - 1 - g] = _flat_scores(k_refs[g][0, 0], q_rep) + (later + carry + lfnew_ref[0])
        carry = carry + total
    carry_ref[...] = carry

    @pl.when(p == n_steps - 1)
    def _():
        s_new = lax.dot_general(knew_ref[0].astype(bf16), q_rep, (((1,), (1,)), ((), ())),
                                preferred_element_type=f32)
        s_new = jnp.sum(jnp.where(_eye(N_HEADS, LANES), s_new, 0.0), axis=0, keepdims=True)
        row = lax.broadcasted_iota(jnp.int32, snew_ref.shape[1:], 0)
        lane = lax.broadcasted_iota(jnp.int32, snew_ref.shape[1:], 1)
        snew_ref[0] = jnp.where((row == 0) & (lane < N_HEADS), s_new, -jnp.inf)


def _dec_scores(page_table, q_rep, cache_k, lf_flat, layer, k_new, lf_new_rep):
    nb, n_pages = page_table.shape
    page = cache_k.shape[2]
    g_pages = SCORE_PAGES_PER_STEP
    n_steps = n_pages // g_pages
    rows = page // T_PER_ROW

    def past(g):
        return lambda b, p, pt: pt[b, n_pages - 1 - (p * g_pages + g)]

    k_specs = [pl.BlockSpec((1, 1, page, N_HEADS, HEAD_DIM),
                            lambda b, p, pt, f=past(g): (layer, f(b, p, pt), 0, 0, 0)) for g in range(g_pages)]
    lf_specs = [pl.BlockSpec((1, 1, rows, LANES),
                             lambda b, p, pt, f=past(g): (layer, f(b, p, pt), 0, 0)) for g in range(g_pages)]
    return pl.pallas_call(
        functools.partial(_dec_scores_kernel, n_steps=n_steps),
        grid_spec=pltpu.PrefetchScalarGridSpec(
            num_scalar_prefetch=1,
            grid=(nb, n_steps),
            in_specs=[pl.BlockSpec((1, LANES, HEAD_DIM), lambda b, p, pt: (b, 0, 0))] + k_specs + lf_specs + [
                pl.BlockSpec((1, N_HEADS, HEAD_DIM), lambda b, p, pt: (b, 0, 0)),
                pl.BlockSpec((1, 1, LANES), lambda b, p, pt: (b, 0, 0)),
            ],
            out_specs=[
                pl.BlockSpec((1, g_pages, rows, LANES), lambda b, p, pt: (b, n_steps - 1 - p, 0, 0)),
                pl.BlockSpec((1, SUBLANES, LANES), lambda b, p, pt: (b, 0, 0)),
            ],
            scratch_shapes=[pltpu.VMEM((1, LANES), f32)],
        ),
        out_shape=[jax.ShapeDtypeStruct((nb, n_pages, rows, LANES), f32),
                   jax.ShapeDtypeStruct((nb, SUBLANES, LANES), f32)],
        compiler_params=_params("parallel", "arbitrary"),
        name="dec_scores",
    )(page_table, q_rep, *([cache_k] * g_pages), *([lf_flat] * g_pages), k_new, lf_new_rep)


def _lane_broadcast_rows(flat):
    r = flat.shape[0]
    spread = jnp.where(_eye(LANES, LANES)[None], jnp.broadcast_to(flat[:, None, :], (r, LANES, LANES)), 0.0)
    ones = jnp.ones((LANES, HEAD_DIM), bf16)
    return jnp.dot(spread.reshape(r * LANES, LANES).astype(bf16), ones, preferred_element_type=f32)


def _dec_pv_kernel(pt_ref, s_ref, snew_ref, *refs, n_steps):
    g_pages = PV_PAGES_PER_STEP
    v_refs = refs[:g_pages]
    vnew_ref, gate_ref, o_ref, p_ref, pnew_ref, acc_ref = refs[g_pages:]
    p = pl.program_id(1)

    @pl.when(p == 0)
    def _():
        s = s_ref[0]
        s_new = snew_ref[0]
        m = jnp.maximum(jnp.max(jnp.max(s, axis=0), axis=0, keepdims=True), jnp.max(s_new, axis=0, keepdims=True))
        m = _lane_allreduce(m, jnp.maximum)
        e = jnp.exp(s - m[None])
        e_new = jnp.exp(s_new - m)
        denom = jnp.sum(jnp.sum(e, axis=0), axis=0, keepdims=True) + jnp.sum(e_new, axis=0, keepdims=True)
        denom = _lane_allreduce(denom, jnp.add)
        p_ref[...] = e / denom[None]
        pnew_ref[...] = e_new / denom
        acc_ref[...] = jnp.zeros_like(acc_ref)

    acc = acc_ref[...]
    for g in range(g_pages):
        v3 = v_refs[g][0, 0]
        pb = _lane_broadcast_rows(p_ref[p * g_pages + g])
        acc = acc + jnp.sum(pb.reshape(v3.shape) * v3, axis=0)
    acc_ref[...] = acc

    @pl.when(p == n_steps - 1)
    def _():
        p_new = jnp.where(_eye(N_HEADS, LANES), jnp.broadcast_to(pnew_ref[0:1, :], (N_HEADS, LANES)), 0.0)
        pb = jnp.dot(p_new.astype(bf16), jnp.ones((LANES, HEAD_DIM), bf16), preferred_element_type=f32)
        o_ref[0] = (acc + pb * vnew_ref[0]) * gate_ref[0]


def _dec_pv(page_table, s_all, s_new, cache_v, layer, v_new, gate):
    nb, n_pages = page_table.shape
    page = cache_v.shape[2]
    g_pages = PV_PAGES_PER_STEP
    n_steps = n_pages // g_pages
    rows = page // T_PER_ROW
    head_blk = lambda: pl.BlockSpec((1, N_HEADS, HEAD_DIM), lambda b, p, pt: (b, 0, 0))
    v_specs = [pl.BlockSpec((1, 1, page, N_HEADS, HEAD_DIM),
                            lambda b, p, pt, g=g: (layer, pt[b, p * g_pages + g], 0, 0, 0)) for g in range(g_pages)]
    return pl.pallas_call(
        functools.partial(_dec_pv_kernel, n_steps=n_steps),
        grid_spec=pltpu.PrefetchScalarGridSpec(
            num_scalar_prefetch=1,
            grid=(nb, n_steps),
            in_specs=[
                pl.BlockSpec((1, n_pages, rows, LANES), lambda b, p, pt: (b, 0, 0, 0)),
                pl.BlockSpec((1, SUBLANES, LANES), lambda b, p, pt: (b, 0, 0)),
            ] + v_specs + [head_blk(), head_blk()],
            out_specs=head_blk(),
            scratch_shapes=[pltpu.VMEM((n_pages, rows, LANES), f32), pltpu.VMEM((SUBLANES, LANES), f32),
                            pltpu.VMEM((N_HEADS, HEAD_DIM), f32)],
        ),
        out_shape=jax.ShapeDtypeStruct((nb, N_HEADS, HEAD_DIM), f32),
        compiler_params=_params("parallel", "arbitrary"),
        name="dec_pv",
    )(page_table, s_all, s_new, *([cache_v] * g_pages), v_new, gate)


def _final_norm_kernel(x_ref, g_ref, o_ref):
    o_ref[...] = _rms_rows(x_ref[...], g_ref[...])


def _final_norm(x, g, *, tm):
    m, d = x.shape
    return pl.pallas_call(
        _final_norm_kernel,
        grid=(m // tm,),
        in_specs=[pl.BlockSpec((tm, d), lambda i: (i, 0)), pl.BlockSpec((1, d), lambda i: (0, 0))],
        out_specs=pl.BlockSpec((tm, d), lambda i: (i, 0)),
        out_shape=jax.ShapeDtypeStruct((m, d), f32),
        compiler_params=_params("parallel"),
        name="final_norm",
    )(x, g.reshape(1, d))


def kernel(x_prompt, x_sample, cache_k, cache_v, cache_logf, page_table, norm_ffn1, ffn1_w_up, ffn1_w_down, norm_mix, norm_ffn2, ffn2_w_up, ffn2_w_down, gm_w_in, gm_ln_g, gm_ln_b, gm_w_s, gm_b_s, gm_w_out, fox_w_in, fox_b_f, fox_q_norm, fox_k_norm, fox_w_out, norm_final):
    bp, lp, d = x_prompt.shape
    bs, ls, _ = x_sample.shape
    assert ls == 1 and lp % FLASH_TILE == 0 and d == N_HEADS * HEAD_DIM
    depth = norm_ffn1.shape[0]
    mp, ms = bp * lp, bs * ls
    tm_p = min(ROW_TILE, mp)
    tm_down = min(ROW_TILE // 2, mp)
    half = gm_w_in.shape[2] // 2

    xp = x_prompt.reshape(mp, d)
    xs = x_sample.reshape(ms, d)
    lf_p, k_s, v_s, lf_s, gv_s = [], [], [], [], []
    kp_all = vp_all = None
    n_fox = fox_w_in.shape[0]
    n_pool, page = cache_logf.shape[1:3]
    assert page == LANES and page_table.shape[1] % max(SCORE_PAGES_PER_STEP, PV_PAGES_PER_STEP) == 0
    fox_w_t = jnp.swapaxes(fox_w_in, 1, 2)
    lf_flat = cache_logf.reshape(n_fox, n_pool, page // T_PER_ROW, LANES)

    for i in range(depth):
        xp, xs = _ffn(xp, xs, norm_ffn1[i], ffn1_w_up, ffn1_w_down, i, tm_up=tm_p, tm_down=tm_down)
        j = i // 2
        if i % 2 == 0:
            zp, zs = _gelu_proj(xp, xs, norm_mix[i], gm_w_in, j, tm=tm_p)
            gp = _gmlp_mix(zp, gm_ln_g[j], gm_ln_b[j], gm_w_s[j], gm_b_s[j].T)
            w00 = jnp.repeat(gm_w_s[j, :, 0, 0], GMLP_GROUP)
            b0 = jnp.repeat(gm_b_s[j, :, 0], GMLP_GROUP)
            gs, gv = _gmlp_mix_first_row(zs, gm_ln_g[j], gm_ln_b[j], w00, b0)
            gv_s.append(gv.reshape(bs, ls, half))
            xp, xs = _mm_res(gp, gs, gm_w_out, j, xp, xs, 1.0, tm=tm_p)
        else:
            w_ft = fox_w_t[j, 4 * d:, :]
            qp, kp_all, kp16, vp_all, vp16, gatep, qs, ks, vs, gates = _fox_proj(
                xp, xs, norm_mix[i], fox_w_t, j, fox_q_norm[j], fox_k_norm[j], kp_all, vp_all, n_fox,
                tm=tm_p, q_scale=SCORE_SCALE * LOG2E, qs_scale=SCORE_SCALE)
            lfp = _logf_proj(xp, norm_mix[i], w_ft, fox_b_f[j], tm=tm_p).reshape(bp, lp, N_HEADS)
            og = _flash_prompt(qp.reshape(bp, lp, d), kp16.reshape(bp, lp, d), vp16.reshape(bp, lp, d),
                               _cumsum_seq(lfp), gatep.reshape(bp, lp, d))
            lf_p.append(lfp)
            lfs = _logf_proj(xs, norm_mix[i], w_ft, fox_b_f[j], tm=ms)
            q_rep = jnp.tile(qs.reshape(bs, N_HEADS, HEAD_DIM), (1, T_PER_ROW, 1))
            lf_new_rep = jnp.tile(lfs, (1, T_PER_ROW)).reshape(bs, 1, LANES)
            ks3 = ks.reshape(bs, N_HEADS, HEAD_DIM)
            vs3 = vs.reshape(bs, N_HEADS, HEAD_DIM)
            k_s.append(ks3.reshape(bs, ls, N_HEADS, HEAD_DIM))
            v_s.append(vs3.reshape(bs, ls, N_HEADS, HEAD_DIM))
            s_all, s_new = _dec_scores(page_table, q_rep, cache_k, lf_flat, j, ks3, lf_new_rep)
            os_ = _dec_pv(page_table, s_all, s_new, cache_v, j, vs3, gates.reshape(bs, N_HEADS, HEAD_DIM))
            lf_s.append(lfs.reshape(bs, ls, N_HEADS))
            xp, xs = _mm_res(og.reshape(mp, d), os_.reshape(ms, d).astype(bf16), fox_w_out, j, xp, xs, 1.0, tm=tm_p)
        xp, xs = _ffn(xp, xs, norm_ffn2[i], ffn2_w_up, ffn2_w_down, i, tm_up=tm_p, tm_down=tm_down)

    y_prompt = _final_norm(xp, norm_final, tm=tm_down).reshape(bp, lp, d)
    y_sample = _final_norm(xs, norm_final, tm=ms).reshape(bs, ls, d)
    heads_p = (n_fox, bp, lp, N_HEADS, HEAD_DIM)
    return (y_prompt, y_sample, kp_all.reshape(heads_p), vp_all.reshape(heads_p), jnp.stack(lf_p),
            jnp.stack(k_s), jnp.stack(v_s), jnp.stack(lf_s), jnp.stack(gv_s))
```

```python
import functools

import jax
import jax.numpy as jnp
from jax import lax
from jax.experimental import pallas as pl
from jax.experimental.pallas import tpu as pltpu

f32 = jnp.float32
bf16 = jnp.bfloat16

N_HEADS = 16
HEAD_DIM = 128
CHUNK = 128
GMLP_GROUP = 128
NORM_EPS = 1e-6
LN_EPS = 1e-5
FFN_RESIDUAL = 0.5
LANES = 128
SUBLANES = 8
MXU_COLS = 256
ROW_TILE = 1024
FLASH_TILE = 512

VMEM_LIMIT_BYTES = 56 * 1024 * 1024
HIGHEST = lax.Precision.HIGHEST
LOG2E = 1.4426950408889634
SCORE_SCALE = HEAD_DIM ** -0.5


def _params(*sem):
    return pltpu.CompilerParams(dimension_semantics=sem, vmem_limit_bytes=VMEM_LIMIT_BYTES)


def _rms_rows(x, g):
    ms = jnp.mean(x * x, axis=-1, keepdims=True)
    return x * lax.rsqrt(ms + NORM_EPS) * g


def _sigmoid(x):
    return 1.0 / (1.0 + jnp.exp(-x))


SAMPLE_ROWS = 16


def _ffn_up_kernel(x_ref, xs_ref, g_ref, wa_ref, wb_ref, o_ref, os_ref, xn_ref, *, nb, tn, nvalid, sub):
    j = pl.program_id(1)
    tm = x_ref.shape[0]
    ms = xs_ref.shape[0]

    @pl.when(j == 0)
    def _():
        xn_ref[:tm, :] = _rms_rows(x_ref[...], g_ref[...]).astype(bf16)
        xsn = _rms_rows(xs_ref[...], g_ref[...])
        pad = jnp.zeros((SAMPLE_ROWS - ms, xsn.shape[1]), f32)
        xn_ref[tm:, :] = jnp.concatenate([xsn, pad], axis=0).astype(bf16)

    def emit(ncols, b_off):
        xn = xn_ref[...]
        for c0 in range(0, ncols, sub):
            c1 = min(c0 + sub, ncols)
            wa = wa_ref[0, :, c0:c1].astype(bf16)
            wb = wb_ref[0, :, b_off + c0:b_off + c1].astype(bf16)
            if 2 * (c1 - c0) <= MXU_COLS:
                ab = jnp.dot(xn, jnp.concatenate([wa, wb], axis=1), preferred_element_type=f32)
                a, b = ab[:, :c1 - c0], ab[:, c1 - c0:]
            else:
                a = jnp.dot(xn, wa, preferred_element_type=f32)
                b = jnp.dot(xn, wb, preferred_element_type=f32)
            h = (a * _sigmoid(a) * b).astype(o_ref.dtype)
            o_ref[:, c0:c1] = h[:tm]
            os_ref[:, c0:c1] = h[tm:]
        if ncols < tn:
            os_ref[:, ncols:] = jnp.zeros((SAMPLE_ROWS, tn - ncols), os_ref.dtype)

    if nvalid == tn:
        emit(tn, 0)
    else:
        @pl.when(j < nb - 1)
        def _():
            emit(tn, 0)

        @pl.when(j == nb - 1)
        def _():
            emit(nvalid, tn - nvalid)


def _ffn_up(x, xs, g, w_up, layer, *, tm, tn=512, sub=256):
    m, d = x.shape
    ms = xs.shape[0]
    assert ms <= SAMPLE_ROWS
    dff = w_up.shape[2] // 2
    nb = pl.cdiv(dff, tn)
    nvalid = dff - (nb - 1) * tn
    kern = functools.partial(_ffn_up_kernel, nb=nb, tn=tn, nvalid=nvalid, sub=sub)
    h, hs = pl.pallas_call(
        kern,
        grid=(m // tm, nb),
        in_specs=[
            pl.BlockSpec((tm, d), lambda i, j: (i, 0)),
            pl.BlockSpec((ms, d), lambda i, j: (0, 0)),
            pl.BlockSpec((1, d), lambda i, j: (0, 0)),
            pl.BlockSpec((1, d, tn), lambda i, j: (layer, 0, j)),
            pl.BlockSpec((pl.Element(1), pl.Element(d), pl.Element(tn)),
                         lambda i, j: (layer, 0, pl.multiple_of(jnp.minimum(dff + j * tn, 2 * dff - tn), LANES))),
        ],
        out_specs=[pl.BlockSpec((tm, tn), lambda i, j: (i, j)),
                   pl.BlockSpec((SAMPLE_ROWS, tn), lambda i, j: (0, jnp.where(i == 0, j, nb)))],
        out_shape=[jax.ShapeDtypeStruct((m, dff), bf16),
                   jax.ShapeDtypeStruct((SAMPLE_ROWS, (nb + 1) * tn), bf16)],
        scratch_shapes=[pltpu.VMEM((tm + SAMPLE_ROWS, d), bf16)],
        compiler_params=_params("arbitrary", "arbitrary"),
        name="ffn_up",
    )(x, xs, g.reshape(1, d), w_up, w_up)
    return h, hs[:ms, :dff]


def _mm_res_kernel(h_ref, hs_ref, w_ref, r_ref, rs_ref, o_ref, os_ref, wb_ref, *, scale):
    @pl.when(pl.program_id(1) == 0)
    def _():
        wb_ref[...] = w_ref[0].astype(bf16)
        os_ref[...] = rs_ref[...] + scale * jnp.dot(hs_ref[...], wb_ref[...], preferred_element_type=f32)

    acc = jnp.dot(h_ref[...], wb_ref[...], preferred_element_type=f32)
    o_ref[...] = r_ref[...] + scale * acc


def _mm_res(h, hs, w, layer, res, res_s, scale, *, tm, tn=512):
    m, k = h.shape
    ms = hs.shape[0]
    n = w.shape[2]
    return pl.pallas_call(
        functools.partial(_mm_res_kernel, scale=scale),
        grid=(n // tn, m // tm),
        in_specs=[
            pl.BlockSpec((tm, k), lambda j, i: (i, 0)),
            pl.BlockSpec((ms, k), lambda j, i: (0, 0)),
            pl.BlockSpec((1, k, tn), lambda j, i: (layer, 0, j)),
            pl.BlockSpec((tm, tn), lambda j, i: (i, j)),
            pl.BlockSpec((ms, tn), lambda j, i: (0, j)),
        ],
        out_specs=[pl.BlockSpec((tm, tn), lambda j, i: (i, j)), pl.BlockSpec((ms, tn), lambda j, i: (0, j))],
        out_shape=[jax.ShapeDtypeStruct((m, n), f32), jax.ShapeDtypeStruct((ms, n), f32)],
        scratch_shapes=[pltpu.VMEM((k, tn), bf16)],
        compiler_params=_params("parallel", "arbitrary"),
        name="mm_res",
    )(h, hs, w, res, res_s)


def _ffn(x, xs, g, w_up, w_down, layer, *, tm_up, tm_down):
    h, hs = _ffn_up(x, xs, g, w_up, layer, tm=tm_up)
    return _mm_res(h, hs, w_down, layer, x, xs, FFN_RESIDUAL, tm=tm_down)


def _head_rms(acc, gain, tn):
    outs = []
    for hh in range(tn // HEAD_DIM):
        blk = acc[:, hh * HEAD_DIM:(hh + 1) * HEAD_DIM]
        ms = jnp.mean(blk * blk, axis=-1, keepdims=True)
        outs.append(blk * lax.rsqrt(ms + NORM_EPS) * gain)
    return jnp.concatenate(outs, axis=1)


def _gelu_proj_kernel(x_ref, xs_ref, g_ref, w_ref, o_ref, os_ref, xn_ref):
    tm = x_ref.shape[0]
    ms = xs_ref.shape[0]

    @pl.when(pl.program_id(1) == 0)
    def _():
        xn_ref[:tm, :] = _rms_rows(x_ref[...], g_ref[...]).astype(bf16)
        xsn = _rms_rows(xs_ref[...], g_ref[...])
        pad = jnp.zeros((SAMPLE_ROWS - ms, xsn.shape[1]), f32)
        xn_ref[tm:, :] = jnp.concatenate([xsn, pad], axis=0).astype(bf16)

    acc = jnp.dot(xn_ref[...], w_ref[0].astype(bf16), preferred_element_type=f32)
    z = 0.5 * acc * (1.0 + lax.erf(acc * (0.5 ** 0.5)))
    o_ref[...] = z[:tm].astype(o_ref.dtype)
    os_ref[...] = z[tm:]


def _gelu_proj(x, xs, g, w, layer, *, tm, tn=512):
    m, d = x.shape
    ms = xs.shape[0]
    assert ms <= SAMPLE_ROWS
    ncols = w.shape[2]
    nbn = ncols // tn
    out, out_s = pl.pallas_call(
        _gelu_proj_kernel,
        grid=(m // tm, nbn),
        in_specs=[pl.BlockSpec((tm, d), lambda i, j: (i, 0)), pl.BlockSpec((ms, d), lambda i, j: (0, 0)),
                  pl.BlockSpec((1, d), lambda i, j: (0, 0)), pl.BlockSpec((1, d, tn), lambda i, j: (layer, 0, j))],
        out_specs=[pl.BlockSpec((tm, tn), lambda i, j: (i, j)),
                   pl.BlockSpec((SAMPLE_ROWS, tn), lambda i, j: (0, jnp.where(i == 0, j, nbn)))],
        out_shape=[jax.ShapeDtypeStruct((m, ncols), bf16), jax.ShapeDtypeStruct((SAMPLE_ROWS, ncols + tn), f32)],
        scratch_shapes=[pltpu.VMEM((tm + SAMPLE_ROWS, d), bf16)],
        compiler_params=_params("arbitrary", "arbitrary"),
        name="gelu_proj",
    )(x, xs, g.reshape(1, d), w)
    return out, out_s[:ms, :ncols]


def _fox_proj_kernel(x_ref, xs_ref, g_ref, w_ref, qg_ref, kg_ref, *refs, tn, nseg, q_scale, qs_scale):
    q_ref, k_ref, k16_ref, v_ref, v16_ref, gate_ref, qs_ref, ks_ref, vs_ref, gs_ref, xn_ref = refs[-11:]
    j = pl.program_id(1)
    tm = x_ref.shape[0]
    ms = xs_ref.shape[0]

    @pl.when(j == 0)
    def _():
        xn_ref[:tm, :] = _rms_rows(x_ref[...], g_ref[...]).astype(bf16)
        xsn = _rms_rows(xs_ref[...], g_ref[...])
        pad = jnp.zeros((SAMPLE_ROWS - ms, xsn.shape[1]), f32)
        xn_ref[tm:, :] = jnp.concatenate([xsn, pad], axis=0).astype(bf16)

    acc = lax.dot_general(xn_ref[...], w_ref[0].astype(bf16), (((1,), (1,)), ((), ())), preferred_element_type=f32)
    seg = j // nseg

    @pl.when(seg == 0)
    def _():
        qn = _head_rms(acc, qg_ref[...], tn)
        q_ref[...] = (qn[:tm] * q_scale).astype(bf16)
        qs_ref[...] = (qn[tm:] * qs_scale).astype(bf16)

    def store_heads(ref, rows):
        hpt = tn // HEAD_DIM
        rows = rows.reshape(tm, hpt, HEAD_DIM)
        for part in range(ref.shape[1] // hpt):
            @pl.when(j % (ref.shape[1] // hpt) == part)
            def _():
                ref[:, part * hpt:(part + 1) * hpt, :] = rows

    @pl.when(seg == 1)
    def _():
        kn = _head_rms(acc, kg_ref[...], tn)
        store_heads(k_ref, kn[:tm])
        k16_ref[...] = kn[:tm].astype(bf16)
        ks_ref[...] = kn[tm:]

    @pl.when(seg == 2)
    def _():
        store_heads(v_ref, acc[:tm])
        v16_ref[...] = acc[:tm].astype(bf16)
        vs_ref[...] = acc[tm:]

    @pl.when(seg == 3)
    def _():
        gate = _sigmoid(acc)
        gate_ref[...] = gate[:tm]
        gs_ref[...] = gate[tm:]


def _fox_proj(x, xs, g, w_t, layer, q_gain, k_gain, k_all, v_all, n_slots, *, tm, q_scale, qs_scale, tn=512):
    m, d = x.shape
    ms = xs.shape[0]
    assert ms <= SAMPLE_ROWS
    nseg = d // tn

    def window(seg):
        return lambda i, j: (i, jnp.clip(j - seg * nseg, 0, nseg - 1))

    tiles_per_blk = SUBLANES * HEAD_DIM // tn
    assert nseg % tiles_per_blk == 0

    def slot_window(seg):
        return lambda i, j: (layer, i, jnp.clip(j - seg * nseg, 0, nseg - 1) // tiles_per_blk, 0)

    def sample_window(seg):
        return lambda i, j: (0, jnp.where(i == 0, jnp.clip(j - seg * nseg, 0, nseg - 1), nseg))

    in_specs = [pl.BlockSpec((tm, d), lambda i, j: (i, 0), pipeline_mode=pl.Buffered(1)),
                pl.BlockSpec((ms, d), lambda i, j: (0, 0)),
                pl.BlockSpec((1, d), lambda i, j: (0, 0)), pl.BlockSpec((1, tn, d), lambda i, j: (layer, j, 0)),
                pl.BlockSpec((1, HEAD_DIM), lambda i, j: (0, 0)), pl.BlockSpec((1, HEAD_DIM), lambda i, j: (0, 0))]
    args = [x, xs, g.reshape(1, d), w_t, q_gain.reshape(1, HEAD_DIM), k_gain.reshape(1, HEAD_DIM)]
    aliases = {}
    for buf, out_idx in ((k_all, 1), (v_all, 3)):
        if buf is not None:
            in_specs.append(pl.BlockSpec(memory_space=pl.ANY))
            args.append(buf)
            aliases[len(args) - 1] = out_idx
    blk = lambda seg: pl.BlockSpec((tm, tn), window(seg))
    slot_blk = lambda seg: pl.BlockSpec((None, tm, SUBLANES, HEAD_DIM), slot_window(seg))
    sample_blk = lambda seg: pl.BlockSpec((SAMPLE_ROWS, tn), sample_window(seg))
    full = lambda dt: jax.ShapeDtypeStruct((m, d), dt)
    slots = jax.ShapeDtypeStruct((n_slots, m, d // HEAD_DIM, HEAD_DIM), f32)
    sample = lambda dt: jax.ShapeDtypeStruct((SAMPLE_ROWS, d + tn), dt)
    q, k_all, k16, v_all, v16, gate, q_s, k_s, v_s, gate_s = pl.pallas_call(
        functools.partial(_fox_proj_kernel, tn=tn, nseg=nseg, q_scale=q_scale, qs_scale=qs_scale),
        grid=(m // tm, 4 * nseg),
        in_specs=in_specs,
        out_specs=[blk(0), slot_blk(1), blk(1), slot_blk(2), blk(2), blk(3),
                   sample_blk(0), sample_blk(1), sample_blk(2), sample_blk(3)],
        out_shape=[full(bf16), slots, full(bf16), slots, full(bf16), full(f32),
                   sample(bf16), sample(f32), sample(f32), sample(f32)],
        input_output_aliases=aliases,
        scratch_shapes=[pltpu.VMEM((tm + SAMPLE_ROWS, d), bf16)],
        compiler_params=_params("arbitrary", "arbitrary"),
        name="fox_proj",
    )(*args)
    cut = lambda a: a[:ms, :d]
    return q, k_all, k16, v_all, v16, gate, cut(q_s), cut(k_s), cut(v_s), cut(gate_s)


def _logf_kernel(x_ref, g_ref, w_ref, b_ref, o_ref):
    xn = _rms_rows(x_ref[...], g_ref[...]).astype(bf16)
    z = lax.dot_general(xn, w_ref[...].astype(bf16), (((1,), (1,)), ((), ())),
                        preferred_element_type=f32) + b_ref[...]
    o_ref[...] = jnp.minimum(z, 0.0) - jnp.log1p(jnp.exp(-jnp.abs(z)))


def _logf_proj(x, g, w_ft, b_f, *, tm):
    m, d = x.shape
    return pl.pallas_call(
        _logf_kernel,
        grid=(m // tm,),
        in_specs=[
            pl.BlockSpec((tm, d), lambda i: (i, 0)),
            pl.BlockSpec((1, d), lambda i: (0, 0)),
            pl.BlockSpec((N_HEADS, d), lambda i: (0, 0)),
            pl.BlockSpec((1, N_HEADS), lambda i: (0, 0)),
        ],
        out_specs=pl.BlockSpec((tm, N_HEADS), lambda i: (i, 0)),
        out_shape=jax.ShapeDtypeStruct((m, N_HEADS), f32),
        compiler_params=_params("parallel"),
        name="logf_proj",
    )(x, g.reshape(1, d), w_ft, b_f.reshape(1, N_HEADS))


def _layer_norm_rows(v, g, b):
    mu = jnp.mean(v, axis=-1, keepdims=True)
    vc = v - mu
    var = jnp.mean(vc * vc, axis=-1, keepdims=True)
    return vc * lax.rsqrt(var + LN_EPS) * g + b


def _gmlp_mix_kernel(u_ref, v_ref, g_ref, b_ref, ws_ref, bs_ref, o_ref, *, n_groups, n_chunks):
    vn = _layer_norm_rows(v_ref[...].astype(f32), g_ref[...], b_ref[...]).astype(bf16)
    row = lax.broadcasted_iota(jnp.int32, (CHUNK, CHUNK), 0)
    col = lax.broadcasted_iota(jnp.int32, (CHUNK, CHUNK), 1)
    causal = row >= col
    for gi in range(n_groups):
        cols = slice(gi * GMLP_GROUP, (gi + 1) * GMLP_GROUP)
        ws = jnp.where(causal, ws_ref[gi], 0.0).astype(bf16)
        rhs = jnp.concatenate([vn[c * CHUNK:(c + 1) * CHUNK, cols] for c in range(n_chunks)], axis=1)
        mixed = jnp.dot(ws, rhs, preferred_element_type=f32) + bs_ref[:, gi:gi + 1]
        for c in range(n_chunks):
            rows = slice(c * CHUNK, (c + 1) * CHUNK)
            u = u_ref[rows, cols].astype(f32)
            o_ref[rows, cols] = (u * mixed[:, c * GMLP_GROUP:(c + 1) * GMLP_GROUP]).astype(bf16)


def _gmlp_mix(z, ln_g, ln_b, w_s, b_s_t, *, n_chunks=2):
    m = z.shape[0]
    half = z.shape[1] // 2
    n_groups = half // GMLP_GROUP
    tm = n_chunks * CHUNK
    return pl.pallas_call(
        functools.partial(_gmlp_mix_kernel, n_groups=n_groups, n_chunks=n_chunks),
        grid=(m // tm,),
        in_specs=[
            pl.BlockSpec((tm, half), lambda i: (i, 0)),
            pl.BlockSpec((tm, half), lambda i: (i, 1)),
            pl.BlockSpec((1, half), lambda i: (0, 0)),
            pl.BlockSpec((1, half), lambda i: (0, 0)),
            pl.BlockSpec((n_groups, CHUNK, CHUNK), lambda i: (0, 0, 0)),
            pl.BlockSpec((CHUNK, n_groups), lambda i: (0, 0)),
        ],
        out_specs=pl.BlockSpec((tm, half), lambda i: (i, 0)),
        out_shape=jax.ShapeDtypeStruct((m, half), bf16),
        compiler_params=_params("parallel"),
        name="gmlp_mix",
    )(z, z, ln_g.reshape(1, half), ln_b.reshape(1, half), w_s, b_s_t)


def _gmlp_mix_first_row_kernel(u_ref, v_ref, g_ref, b_ref, w0_ref, b0_ref, o_ref, vn_ref):
    vn = _layer_norm_rows(v_ref[...], g_ref[...], b_ref[...])
    vn_ref[...] = vn
    mixed = vn * w0_ref[...] + b0_ref[...]
    o_ref[...] = (u_ref[...] * mixed).astype(bf16)


def _gmlp_mix_first_row(z, ln_g, ln_b, w00, b0):
    m = z.shape[0]
    half = z.shape[1] // 2
    row = lambda: pl.BlockSpec((1, half), lambda i: (0, 0))
    return pl.pallas_call(
        _gmlp_mix_first_row_kernel,
        grid=(1,),
        in_specs=[
            pl.BlockSpec((m, half), lambda i: (0, 0)),
            pl.BlockSpec((m, half), lambda i: (0, 1)),
            row(), row(), row(), row(),
        ],
        out_specs=[pl.BlockSpec((m, half), lambda i: (0, 0)), pl.BlockSpec((m, half), lambda i: (0, 0))],
        out_shape=[jax.ShapeDtypeStruct((m, half), bf16), jax.ShapeDtypeStruct((m, half), f32)],
        compiler_params=_params("arbitrary"),
        name="gmlp_mix_first_row",
    )(z, z, ln_g.reshape(1, half), ln_b.reshape(1, half), w00.reshape(1, half), b0.reshape(1, half))


def _cumsum_kernel(x_ref, o_ref, *, n_blocks):
    row = lax.broadcasted_iota(jnp.int32, (CHUNK, CHUNK), 0)
    col = lax.broadcasted_iota(jnp.int32, (CHUNK, CHUNK), 1)
    tri = (row >= col).astype(f32)
    carry = jnp.zeros((1, N_HEADS), f32)
    for blk in range(n_blocks):
        rows = slice(blk * CHUNK, (blk + 1) * CHUNK)
        cs = jnp.dot(tri, x_ref[0, rows, :], precision=HIGHEST, preferred_element_type=f32) + carry
        o_ref[0, rows, :] = cs
        carry = cs[CHUNK - 1:CHUNK, :]


def _cumsum_seq(lf):
    b, l, h = lf.shape
    return pl.pallas_call(
        functools.partial(_cumsum_kernel, n_blocks=l // CHUNK),
        grid=(b,),
        in_specs=[pl.BlockSpec((1, l, h), lambda i: (i, 0, 0))],
        out_specs=pl.BlockSpec((1, l, h), lambda i: (i, 0, 0)),
        out_shape=jax.ShapeDtypeStruct((b, l, h), f32),
        compiler_params=_params("parallel"),
        name="cumsum_seq",
    )(lf)


def _split3(x):
    hi = x.astype(bf16).astype(f32)
    r = x - hi
    mid = r.astype(bf16).astype(f32)
    lo = (r - mid).astype(bf16).astype(f32)
    return hi, mid, lo


def _flash_kernel(q_ref, k_ref, v_ref, c_ref, gate_ref, o_ref, kaug_ref, vaug_ref, qside_ref, s_ref, *, tq):
    h = pl.program_id(1)
    qi = pl.program_id(2)
    seq = k_ref.shape[1]

    @pl.when(qi == 0)
    def _():
        head = lax.broadcasted_iota(jnp.int32, (seq, N_HEADS), 1)
        c2 = jnp.sum(jnp.where(head == h, c_ref[0], 0.0), axis=1, keepdims=True) * LOG2E
        hi, mid, lo = _split3(c2)
        lane = lax.broadcasted_iota(jnp.int32, (seq, HEAD_DIM), 1)
        qside = jnp.where(lane == 0, hi, jnp.where(lane == 1, mid, jnp.where(lane == 2, lo,
                          jnp.where(lane < 6, 1.0, 0.0))))
        kside = jnp.where(lane < 3, 1.0, jnp.where(lane == 3, -hi, jnp.where(lane == 4, -mid,
                          jnp.where(lane == 5, -lo, 0.0))))
        qside_ref[...] = qside.astype(bf16)
        kaug_ref[:, :HEAD_DIM] = k_ref[0]
        kaug_ref[:, HEAD_DIM:] = kside.astype(bf16)
        vaug_ref[:, :HEAD_DIM] = v_ref[0]
        vaug_ref[:, HEAD_DIM:] = jnp.where(lane == 0, 1.0, 0.0).astype(bf16)

    q = jnp.concatenate([q_ref[0], qside_ref[pl.ds(pl.multiple_of(qi * tq, tq), tq), :]], axis=1)
    col_minus_row = (lax.broadcasted_iota(jnp.int32, (tq, tq), 1)
                     - lax.broadcasted_iota(jnp.int32, (tq, tq), 0))

    def key_rows(kj):
        return pl.ds(pl.multiple_of(jnp.minimum(kj, qi) * tq, tq), tq)

    def scores(kj):
        return lax.dot_general(q, kaug_ref[key_rows(kj), :], (((1,), (1,)), ((), ())),
                               preferred_element_type=f32)

    def absorb(kj, s, m, acc):
        m_new = jnp.maximum(m, jnp.max(s, axis=1, keepdims=True))
        p = jnp.exp2(s - m_new).astype(bf16)
        pv = jnp.dot(p, vaug_ref[key_rows(kj), :], preferred_element_type=f32)
        return m_new, jnp.exp2(m - m_new) * acc + pv

    s_ref[...] = scores(0)

    def pair(j, carry):
        m, acc = carry
        s_even = s_ref[...]
        s_odd = jnp.where(col_minus_row <= (qi - (2 * j + 1)) * tq, scores(2 * j + 1), -jnp.inf)
        m, acc = absorb(2 * j, s_even, m, acc)
        s_ref[...] = scores(2 * j + 2)
        return absorb(2 * j + 1, s_odd, m, acc)

    init = (jnp.full((tq, 1), -jnp.inf, f32), jnp.zeros((tq, 2 * HEAD_DIM), f32))
    m, acc = lax.fori_loop(0, (qi + 1) // 2, pair, init)

    def finish(acc):
        o = acc[:, :HEAD_DIM] / acc[:, HEAD_DIM:HEAD_DIM + 1]
        o_ref[0] = (o * gate_ref[0]).astype(bf16)

    @pl.when(qi % 2 == 1)
    def _():
        finish(acc)

    @pl.when(qi % 2 == 0)
    def _():
        s = jnp.where(col_minus_row <= 0, s_ref[...], -jnp.inf)
        finish(absorb(qi, s, m, acc)[1])


def _flash_prompt(q, k, v, c, gate, *, tq=FLASH_TILE):
    b, l, d = q.shape
    blk = lambda: pl.BlockSpec((1, tq, HEAD_DIM), lambda bi, h, qi: (bi, qi, h))
    seq = lambda: pl.BlockSpec((1, l, HEAD_DIM), lambda bi, h, qi: (bi, 0, h))
    return pl.pallas_call(
        functools.partial(_flash_kernel, tq=tq),
        grid=(b, N_HEADS, l // tq),
        in_specs=[blk(), seq(), seq(), pl.BlockSpec((1, l, N_HEADS), lambda bi, h, qi: (bi, 0, 0)), blk()],
        out_specs=blk(),
        out_shape=jax.ShapeDtypeStruct((b, l, d), bf16),
        scratch_shapes=[pltpu.VMEM((l, 2 * HEAD_DIM), bf16), pltpu.VMEM((l, 2 * HEAD_DIM), bf16),
                        pltpu.VMEM((l, HEAD_DIM), bf16), pltpu.VMEM((tq, tq), f32)],
        compiler_params=_params("parallel", "parallel", "arbitrary"),
        name="flash_prompt",
    )(q, k, v, c, gate)


T_PER_ROW = LANES // N_HEADS
SCORE_PAGES_PER_STEP = 8
PV_PAGES_PER_STEP = 16


def _eye(n, m):
    return lax.broadcasted_iota(jnp.int32, (n, m), 0) == lax.broadcasted_iota(jnp.int32, (n, m), 1)


def _lane_allreduce(x, op):
    sh = N_HEADS
    while sh < LANES:
        x = op(x, pltpu.roll(x, sh, 1))
        sh *= 2
    return x


def _flat_suffix(x):
    rows = x.shape[0]
    lane = lax.broadcasted_iota(jnp.int32, x.shape, 1)
    row = lax.broadcasted_iota(jnp.int32, x.shape, 0)
    y = x
    sh = N_HEADS
    while sh < LANES:
        y = y + jnp.where(lane < LANES - sh, pltpu.roll(y, LANES - sh, 1), 0.0)
        sh *= 2
    z = _lane_allreduce(jnp.where(lane < N_HEADS, y, 0.0), jnp.add)
    w = z
    sh = 1
    while sh < rows:
        w = w + jnp.where(row < rows - sh, pltpu.roll(w, rows - sh, 0), 0.0)
        sh *= 2
    return (y - x) + (w - z), w[0:1, :]


def _flat_scores(k3, q_rep):
    t = k3.shape[0]
    k2 = k3.reshape(t * N_HEADS, HEAD_DIM).astype(bf16)
    s = lax.dot_general(k2, q_rep, (((1,), (1,)), ((), ())), preferred_element_type=f32)
    s3 = s.reshape(t * N_HEADS // LANES, LANES, LANES)
    return jnp.sum(jnp.where(_eye(LANES, LANES)[None], s3, 0.0), axis=1)


def _dec_scores_kernel(pt_ref, q_ref, *refs, n_steps):
    g_pages = SCORE_PAGES_PER_STEP
    k_refs, lf_refs = refs[:g_pages], refs[g_pages:2 * g_pages]
    knew_ref, lfnew_ref, s_ref, snew_ref, carry_ref = refs[2 * g_pages:]
    p = pl.program_id(1)

    @pl.when(p == 0)
    def _():
        carry_ref[...] = jnp.zeros_like(carry_ref)

    q_rep = q_ref[0]
    carry = carry_ref[...]
    for g in range(g_pages):
        later, total = _flat_suffix(lf_refs[g][0, 0])
        s_ref[0, g_pages - 1 - g] = _flat_scores(k_refs[g][0, 0], q_rep) + (later + carry + lfnew_ref[0])
        carry = carry + total
    carry_ref[...] = carry

    @pl.when(p == n_steps - 1)
    def _():
        s_new = lax.dot_general(knew_ref[0].astype(bf16), q_rep, (((1,), (1,)), ((), ())),
                                preferred_element_type=f32)
        s_new = jnp.sum(jnp.where(_eye(N_HEADS, LANES), s_new, 0.0), axis=0, keepdims=True)
        row = lax.broadcasted_iota(jnp.int32, snew_ref.shape[1:], 0)
        lane = lax.broadcasted_iota(jnp.int32, snew_ref.shape[1:], 1)
        snew_ref[0] = jnp.where((row == 0) & (lane < N_HEADS), s_new, -jnp.inf)


def _dec_scores(page_table, q_rep, cache_k, lf_flat, layer, k_new, lf_new_rep):
    nb, n_pages = page_table.shape
    page = cache_k.shape[2]
    g_pages = SCORE_PAGES_PER_STEP
    n_steps = n_pages // g_pages
    rows = page // T_PER_ROW

    def past(g):
        return lambda b, p, pt: pt[b, n_pages - 1 - (p * g_pages + g)]

    k_specs = [pl.BlockSpec((1, 1, page, N_HEADS, HEAD_DIM),
                            lambda b, p, pt, f=past(g): (layer, f(b, p, pt), 0, 0, 0)) for g in range(g_pages)]
    lf_specs = [pl.BlockSpec((1, 1, rows, LANES),
                             lambda b, p, pt, f=past(g): (layer, f(b, p, pt), 0, 0)) for g in range(g_pages)]
    return pl.pallas_call(
        functools.partial(_dec_scores_kernel, n_steps=n_steps),
        grid_spec=pltpu.PrefetchScalarGridSpec(
            num_scalar_prefetch=1,
            grid=(nb, n_steps),
            in_specs=[pl.BlockSpec((1, LANES, HEAD_DIM), lambda b, p, pt: (b, 0, 0))] + k_specs + lf_specs + [
                pl.BlockSpec((1, N_HEADS, HEAD_DIM), lambda b, p, pt: (b, 0, 0)),
                pl.BlockSpec((1, 1, LANES), lambda b, p, pt: (b, 0, 0)),
            ],
            out_specs=[
                pl.BlockSpec((1, g_pages, rows, LANES), lambda b, p, pt: (b, n_steps - 1 - p, 0, 0)),
                pl.BlockSpec((1, SUBLANES, LANES), lambda b, p, pt: (b, 0, 0)),
            ],
            scratch_shapes=[pltpu.VMEM((1, LANES), f32)],
        ),
        out_shape=[jax.ShapeDtypeStruct((nb, n_pages, rows, LANES), f32),
                   jax.ShapeDtypeStruct((nb, SUBLANES, LANES), f32)],
        compiler_params=_params("parallel", "arbitrary"),
        name="dec_scores",
    )(page_table, q_rep, *([cache_k] * g_pages), *([lf_flat] * g_pages), k_new, lf_new_rep)


def _lane_broadcast_rows(flat):
    r = flat.shape[0]
    spread = jnp.where(_eye(LANES, LANES)[None], jnp.broadcast_to(flat[:, None, :], (r, LANES, LANES)), 0.0)
    ones = jnp.ones((LANES, HEAD_DIM), bf16)
    return jnp.dot(spread.reshape(r * LANES, LANES).astype(bf16), ones, preferred_element_type=f32)


def _dec_pv_kernel(pt_ref, s_ref, snew_ref, *refs, n_steps):
    g_pages = PV_PAGES_PER_STEP
    v_refs = refs[:g_pages]
    vnew_ref, gate_ref, o_ref, p_ref, pnew_ref, acc_ref = refs[g_pages:]
    p = pl.program_id(1)

    @pl.when(p == 0)
    def _():
        s = s_ref[0]
        s_new = snew_ref[0]
        m = jnp.maximum(jnp.max(jnp.max(s, axis=0), axis=0, keepdims=True), jnp.max(s_new, axis=0, keepdims=True))
        m = _lane_allreduce(m, jnp.maximum)
        e = jnp.exp(s - m[None])
        e_new = jnp.exp(s_new - m)
        denom = jnp.sum(jnp.sum(e, axis=0), axis=0, keepdims=True) + jnp.sum(e_new, axis=0, keepdims=True)
        denom = _lane_allreduce(denom, jnp.add)
        p_ref[...] = e / denom[None]
        pnew_ref[...] = e_new / denom
        acc_ref[...] = jnp.zeros_like(acc_ref)

    acc = acc_ref[...]
    for g in range(g_pages):
        v3 = v_refs[g][0, 0]
        pb = _lane_broadcast_rows(p_ref[p * g_pages + g])
        acc = acc + jnp.sum(pb.reshape(v3.shape) * v3, axis=0)
    acc_ref[...] = acc

    @pl.when(p == n_steps - 1)
    def _():
        p_new = jnp.where(_eye(N_HEADS, LANES), jnp.broadcast_to(pnew_ref[0:1, :], (N_HEADS, LANES)), 0.0)
        pb = jnp.dot(p_new.astype(bf16), jnp.ones((LANES, HEAD_DIM), bf16), preferred_element_type=f32)
        o_ref[0] = (acc + pb * vnew_ref[0]) * gate_ref[0]


def _dec_pv(page_table, s_all, s_new, cache_v, layer, v_new, gate):
    nb, n_pages = page_table.shape
    page = cache_v.shape[2]
    g_pages = PV_PAGES_PER_STEP
    n_steps = n_pages // g_pages
    rows = page // T_PER_ROW
    head_blk = lambda: pl.BlockSpec((1, N_HEADS, HEAD_DIM), lambda b, p, pt: (b, 0, 0))
    v_specs = [pl.BlockSpec((1, 1, page, N_HEADS, HEAD_DIM),
                            lambda b, p, pt, g=g: (layer, pt[b, p * g_pages + g], 0, 0, 0)) for g in range(g_pages)]
    return pl.pallas_call(
        functools.partial(_dec_pv_kernel, n_steps=n_steps),
        grid_spec=pltpu.PrefetchScalarGridSpec(
            num_scalar_prefetch=1,
            grid=(nb, n_steps),
            in_specs=[
                pl.BlockSpec((1, n_pages, rows, LANES), lambda b, p, pt: (b, 0, 0, 0)),
                pl.BlockSpec((1, SUBLANES, LANES), lambda b, p, pt: (b, 0, 0)),
            ] + v_specs + [head_blk(), head_blk()],
            out_specs=head_blk(),
            scratch_shapes=[pltpu.VMEM((n_pages, rows, LANES), f32), pltpu.VMEM((SUBLANES, LANES), f32),
                            pltpu.VMEM((N_HEADS, HEAD_DIM), f32)],
        ),
        out_shape=jax.ShapeDtypeStruct((nb, N_HEADS, HEAD_DIM), f32),
        compiler_params=_params("parallel", "arbitrary"),
        name="dec_pv",
    )(page_table, s_all, s_new, *([cache_v] * g_pages), v_new, gate)


def _final_norm_kernel(x_ref, g_ref, o_ref):
    o_ref[...] = _rms_rows(x_ref[...], g_ref[...])


def _final_norm(x, g, *, tm):
    m, d = x.shape
    return pl.pallas_call(
        _final_norm_kernel,
        grid=(m // tm,),
        in_specs=[pl.BlockSpec((tm, d), lambda i: (i, 0)), pl.BlockSpec((1, d), lambda i: (0, 0))],
        out_specs=pl.BlockSpec((tm, d), lambda i: (i, 0)),
        out_shape=jax.ShapeDtypeStruct((m, d), f32),
        compiler_params=_params("parallel"),
        name="final_norm",
    )(x, g.reshape(1, d))


def kernel(x_prompt, x_sample, cache_k, cache_v, cache_logf, page_table, norm_ffn1, ffn1_w_up, ffn1_w_down, norm_mix, norm_ffn2, ffn2_w_up, ffn2_w_down, gm_w_in, gm_ln_g, gm_ln_b, gm_w_s, gm_b_s, gm_w_out, fox_w_in, fox_b_f, fox_q_norm, fox_k_norm, fox_w_out, norm_final):
    bp, lp, d = x_prompt.shape
    bs, ls, _ = x_sample.shape
    assert ls == 1 and lp % FLASH_TILE == 0 and d == N_HEADS * HEAD_DIM
    depth = norm_ffn1.shape[0]
    mp, ms = bp * lp, bs * ls
    tm_p = min(ROW_TILE, mp)
    tm_down = min(ROW_TILE // 2, mp)
    half = gm_w_in.shape[2] // 2

    xp = x_prompt.reshape(mp, d)
    xs = x_sample.reshape(ms, d)
    lf_p, k_s, v_s, lf_s, gv_s = [], [], [], [], []
    kp_all = vp_all = None
    n_fox = fox_w_in.shape[0]
    n_pool, page = cache_logf.shape[1:3]
    assert page == LANES and page_table.shape[1] % max(SCORE_PAGES_PER_STEP, PV_PAGES_PER_STEP) == 0
    fox_w_t = jnp.swapaxes(fox_w_in, 1, 2)
    lf_flat = cache_logf.reshape(n_fox, n_pool, page // T_PER_ROW, LANES)

    for i in range(depth):
        xp, xs = _ffn(xp, xs, norm_ffn1[i], ffn1_w_up, ffn1_w_down, i, tm_up=tm_p, tm_down=tm_down)
        j = i // 2
        if i % 2 == 0:
            zp, zs = _gelu_proj(xp, xs, norm_mix[i], gm_w_in, j, tm=tm_p)
            gp = _gmlp_mix(zp, gm_ln_g[j], gm_ln_b[j], gm_w_s[j], gm_b_s[j].T)
            w00 = jnp.repeat(gm_w_s[j, :, 0, 0], GMLP_GROUP)
            b0 = jnp.repeat(gm_b_s[j, :, 0], GMLP_GROUP)
            gs, gv = _gmlp_mix_first_row(zs, gm_ln_g[j], gm_ln_b[j], w00, b0)
            gv_s.append(gv.reshape(bs, ls, half))
            xp, xs = _mm_res(gp, gs, gm_w_out, j, xp, xs, 1.0, tm=tm_p)
        else:
            w_ft = fox_w_t[j, 4 * d:, :]
            qp, kp_all, kp16, vp_all, vp16, gatep, qs, ks, vs, gates = _fox_proj(
                xp, xs, norm_mix[i], fox_w_t, j, fox_q_norm[j], fox_k_norm[j], kp_all, vp_all, n_fox,
                tm=tm_p, q_scale=SCORE_SCALE * LOG2E, qs_scale=SCORE_SCALE)
            lfp = _logf_proj(xp, norm_mix[i], w_ft, fox_b_f[j], tm=tm_p).reshape(bp, lp, N_HEADS)
            og = _flash_prompt(qp.reshape(bp, lp, d), kp16.reshape(bp, lp, d), vp16.reshape(bp, lp, d),
                               _cumsum_seq(lfp), gatep.reshape(bp, lp, d))
            lf_p.append(lfp)
            lfs = _logf_proj(xs, norm_mix[i], w_ft, fox_b_f[j], tm=ms)
            q_rep = jnp.tile(qs.reshape(bs, N_HEADS, HEAD_DIM), (1, T_PER_ROW, 1))
            lf_new_rep = jnp.tile(lfs, (1, T_PER_ROW)).reshape(bs, 1, LANES)
            ks3 = ks.reshape(bs, N_HEADS, HEAD_DIM)
            vs3 = vs.reshape(bs, N_HEADS, HEAD_DIM)
            k_s.append(ks3.reshape(bs, ls, N_HEADS, HEAD_DIM))
            v_s.append(vs3.reshape(bs, ls, N_HEADS, HEAD_DIM))
            s_all, s_new = _dec_scores(page_table, q_rep, cache_k, lf_flat, j, ks3, lf_new_rep)
            os_ = _dec_pv(page_table, s_all, s_new, cache_v, j, vs3, gates.reshape(bs, N_HEADS, HEAD_DIM))
            lf_s.append(lfs.reshape(bs, ls, N_HEADS))
            xp, xs = _mm_res(og.reshape(mp, d), os_.reshape(ms, d).astype(bf16), fox_w_out, j, xp, xs, 1.0, tm=tm_p)
        xp, xs = _ffn(xp, xs, norm_ffn2[i], ffn2_w_up, ffn2_w_down, i, tm_up=tm_p, tm_down=tm_down)

    y_prompt = _final_norm(xp, norm_final, tm=tm_down).reshape(bp, lp, d)
    y_sample = _final_norm(xs, norm_final, tm=ms).reshape(bs, ls, d)
    heads_p = (n_fox, bp, lp, N_HEADS, HEAD_DIM)
    return (y_prompt, y_sample, kp_all.reshape(heads_p), vp_all.reshape(heads_p), jnp.stack(lf_p),
            jnp.stack(k_s), jnp.stack(v_s), jnp.stack(lf_s), jnp.stack(gv_s))
```

```python
import functools

import jax
import jax.numpy as jnp
from jax import lax
from jax.experimental import pallas as pl
from jax.experimental.pallas import tpu as pltpu

f32 = jnp.float32
bf16 = jnp.bfloat16

N_HEADS = 16
HEAD_DIM = 128
CHUNK = 128
GMLP_GROUP = 128
NORM_EPS = 1e-6
LN_EPS = 1e-5
FFN_RESIDUAL = 0.5
LANES = 128
SUBLANES = 8
MXU_COLS = 256
ROW_TILE = 1024
FLASH_TILE = 512

VMEM_LIMIT_BYTES = 56 * 1024 * 1024
HIGHEST = lax.Precision.HIGHEST
LOG2E = 1.4426950408889634
SCORE_SCALE = HEAD_DIM ** -0.5


def _params(*sem):
    return pltpu.CompilerParams(dimension_semantics=sem, vmem_limit_bytes=VMEM_LIMIT_BYTES)


def _rms_rows(x, g):
    ms = jnp.mean(x * x, axis=-1, keepdims=True)
    return x * lax.rsqrt(ms + NORM_EPS) * g


def _sigmoid(x):
    return 1.0 / (1.0 + jnp.exp(-x))


SAMPLE_ROWS = 16


def _ffn_up_kernel(x_ref, xs_ref, g_ref, wa_ref, wb_ref, o_ref, os_ref, xn_ref, *, nb, tn, nvalid, sub):
    j = pl.program_id(1)
    tm = x_ref.shape[0]
    ms = xs_ref.shape[0]

    @pl.when(j == 0)
    def _():
        xn_ref[:tm, :] = _rms_rows(x_ref[...], g_ref[...]).astype(bf16)
        xsn = _rms_rows(xs_ref[...], g_ref[...])
        pad = jnp.zeros((SAMPLE_ROWS - ms, xsn.shape[1]), f32)
        xn_ref[tm:, :] = jnp.concatenate([xsn, pad], axis=0).astype(bf16)

    def emit(ncols, b_off):
        xn = xn_ref[...]
        for c0 in range(0, ncols, sub):
            c1 = min(c0 + sub, ncols)
            wa = wa_ref[0, :, c0:c1].astype(bf16)
            wb = wb_ref[0, :, b_off + c0:b_off + c1].astype(bf16)
            if 2 * (c1 - c0) <= MXU_COLS:
                ab = jnp.dot(xn, jnp.concatenate([wa, wb], axis=1), preferred_element_type=f32)
                a, b = ab[:, :c1 - c0], ab[:, c1 - c0:]
            else:
                a = jnp.dot(xn, wa, preferred_element_type=f32)
                b = jnp.dot(xn, wb, preferred_element_type=f32)
            h = (a * _sigmoid(a) * b).astype(o_ref.dtype)
            o_ref[:, c0:c1] = h[:tm]
            os_ref[:, c0:c1] = h[tm:]
        if ncols < tn:
            os_ref[:, ncols:] = jnp.zeros((SAMPLE_ROWS, tn - ncols), os_ref.dtype)

    if nvalid == tn:
        emit(tn, 0)
    else:
        @pl.when(j < nb - 1)
        def _():
            emit(tn, 0)

        @pl.when(j == nb - 1)
        def _():
            emit(nvalid, tn - nvalid)


def _ffn_up(x, xs, g, w_up, layer, *, tm, tn=512, sub=256):
    m, d = x.shape
    ms = xs.shape[0]
    assert ms <= SAMPLE_ROWS
    dff = w_up.shape[2] // 2
    nb = pl.cdiv(dff, tn)
    nvalid = dff - (nb - 1) * tn
    kern = functools.partial(_ffn_up_kernel, nb=nb, tn=tn, nvalid=nvalid, sub=sub)
    h, hs = pl.pallas_call(
        kern,
        grid=(m // tm, nb),
        in_specs=[
            pl.BlockSpec((tm, d), lambda i, j: (i, 0)),
            pl.BlockSpec((ms, d), lambda i, j: (0, 0)),
            pl.BlockSpec((1, d), lambda i, j: (0, 0)),
            pl.BlockSpec((1, d, tn), lambda i, j: (layer, 0, j)),
            pl.BlockSpec((pl.Element(1), pl.Element(d), pl.Element(tn)),
                         lambda i, j: (layer, 0, pl.multiple_of(jnp.minimum(dff + j * tn, 2 * dff - tn), LANES))),
        ],
        out_specs=[pl.BlockSpec((tm, tn), lambda i, j: (i, j)),
                   pl.BlockSpec((SAMPLE_ROWS, tn), lambda i, j: (0, jnp.where(i == 0, j, nb)))],
        out_shape=[jax.ShapeDtypeStruct((m, dff), bf16),
                   jax.ShapeDtypeStruct((SAMPLE_ROWS, (nb + 1) * tn), bf16)],
        scratch_shapes=[pltpu.VMEM((tm + SAMPLE_ROWS, d), bf16)],
        compiler_params=_params("arbitrary", "arbitrary"),
        name="ffn_up",
    )(x, xs, g.reshape(1, d), w_up, w_up)
    return h, hs[:ms, :dff]


def _mm_res_kernel(h_ref, hs_ref, w_ref, r_ref, rs_ref, o_ref, os_ref, wb_ref, *, scale):
    @pl.when(pl.program_id(1) == 0)
    def _():
        wb_ref[...] = w_ref[0].astype(bf16)
        os_ref[...] = rs_ref[...] + scale * jnp.dot(hs_ref[...], wb_ref[...], preferred_element_type=f32)

    acc = jnp.dot(h_ref[...], wb_ref[...], preferred_element_type=f32)
    o_ref[...] = r_ref[...] + scale * acc


def _mm_res(h, hs, w, layer, res, res_s, scale, *, tm, tn=512):
    m, k = h.shape
    ms = hs.shape[0]
    n = w.shape[2]
    return pl.pallas_call(
        functools.partial(_mm_res_kernel, scale=scale),
        grid=(n // tn, m // tm),
        in_specs=[
            pl.BlockSpec((tm, k), lambda j, i: (i, 0)),
            pl.BlockSpec((ms, k), lambda j, i: (0, 0)),
            pl.BlockSpec((1, k, tn), lambda j, i: (layer, 0, j)),
            pl.BlockSpec((tm, tn), lambda j, i: (i, j)),
            pl.BlockSpec((ms, tn), lambda j, i: (0, j)),
        ],
        out_specs=[pl.BlockSpec((tm, tn), lambda j, i: (i, j)), pl.BlockSpec((ms, tn), lambda j, i: (0, j))],
        out_shape=[jax.ShapeDtypeStruct((m, n), f32), jax.ShapeDtypeStruct((ms, n), f32)],
        scratch_shapes=[pltpu.VMEM((k, tn), bf16)],
        compiler_params=_params("parallel", "arbitrary"),
        name="mm_res",
    )(h, hs, w, res, res_s)


def _ffn(x, xs, g, w_up, w_down, layer, *, tm_up, tm_down):
    h, hs = _ffn_up(x, xs, g, w_up, layer, tm=tm_up)
    return _mm_res(h, hs, w_down, layer, x, xs, FFN_RESIDUAL, tm=tm_down)


def _head_rms(acc, gain, tn):
    outs = []
    for hh in range(tn // HEAD_DIM):
        blk = acc[:, hh * HEAD_DIM:(hh + 1) * HEAD_DIM]
        ms = jnp.mean(blk * blk, axis=-1, keepdims=True)
        outs.append(blk * lax.rsqrt(ms + NORM_EPS) * gain)
    return jnp.concatenate(outs, axis=1)


def _gelu_proj_kernel(x_ref, xs_ref, g_ref, w_ref, o_ref, os_ref, xn_ref):
    tm = x_ref.shape[0]
    ms = xs_ref.shape[0]

    @pl.when(pl.program_id(1) == 0)
    def _():
        xn_ref[:tm, :] = _rms_rows(x_ref[...], g_ref[...]).astype(bf16)
        xsn = _rms_rows(xs_ref[...], g_ref[...])
        pad = jnp.zeros((SAMPLE_ROWS - ms, xsn.shape[1]), f32)
        xn_ref[tm:, :] = jnp.concatenate([xsn, pad], axis=0).astype(bf16)

    acc = jnp.dot(xn_ref[...], w_ref[0].astype(bf16), preferred_element_type=f32)
    z = 0.5 * acc * (1.0 + lax.erf(acc * (0.5 ** 0.5)))
    o_ref[...] = z[:tm].astype(o_ref.dtype)
    os_ref[...] = z[tm:]


def _gelu_proj(x, xs, g, w, layer, *, tm, tn=512):
    m, d = x.shape
    ms = xs.shape[0]
    assert ms <= SAMPLE_ROWS
    ncols = w.shape[2]
    nbn = ncols // tn
    out, out_s = pl.pallas_call(
        _gelu_proj_kernel,
        grid=(m // tm, nbn),
        in_specs=[pl.BlockSpec((tm, d), lambda i, j: (i, 0)), pl.BlockSpec((ms, d), lambda i, j: (0, 0)),
                  pl.BlockSpec((1, d), lambda i, j: (0, 0)), pl.BlockSpec((1, d, tn), lambda i, j: (layer, 0, j))],
        out_specs=[pl.BlockSpec((tm, tn), lambda i, j: (i, j)),
                   pl.BlockSpec((SAMPLE_ROWS, tn), lambda i, j: (0, jnp.where(i == 0, j, nbn)))],
        out_shape=[jax.ShapeDtypeStruct((m, ncols), bf16), jax.ShapeDtypeStruct((SAMPLE_ROWS, ncols + tn), f32)],
        scratch_shapes=[pltpu.VMEM((tm + SAMPLE_ROWS, d), bf16)],
        compiler_params=_params("arbitrary", "arbitrary"),
        name="gelu_proj",
    )(x, xs, g.reshape(1, d), w)
    return out, out_s[:ms, :ncols]


def _fox_proj_kernel(x_ref, xs_ref, g_ref, w_ref, qg_ref, kg_ref, *refs, tn, nseg, q_scale, qs_scale):
    q_ref, k_ref, k16_ref, v_ref, v16_ref, gate_ref, qs_ref, ks_ref, vs_ref, gs_ref, xn_ref = refs[-11:]
    j = pl.program_id(1)
    tm = x_ref.shape[0]
    ms = xs_ref.shape[0]

    @pl.when(j == 0)
    def _():
        xn_ref[:tm, :] = _rms_rows(x_ref[...], g_ref[...]).astype(bf16)
        xsn = _rms_rows(xs_ref[...], g_ref[...])
        pad = jnp.zeros((SAMPLE_ROWS - ms, xsn.shape[1]), f32)
        xn_ref[tm:, :] = jnp.concatenate([xsn, pad], axis=0).astype(bf16)

    def project():
        return lax.dot_general(xn_ref[...], w_ref[0].astype(bf16), (((1,), (1,)), ((), ())),
                               preferred_element_type=f32)

    seg = j // nseg

    @pl.when(seg == 0)
    def _():
        qn = _head_rms(project(), qg_ref[...], tn)
        q_ref[...] = (qn[:tm] * q_scale).astype(bf16)
        qs_ref[...] = (qn[tm:] * qs_scale).astype(bf16)

    hpt = tn // HEAD_DIM
    parts = k_ref.shape[1] // hpt
    for part in range(parts):
        heads = slice(part * hpt, (part + 1) * hpt)

        @pl.when((seg == 1) & (j % parts == part))
        def _():
            kn = _head_rms(project(), kg_ref[...], tn)
            k_ref[:, heads, :] = kn[:tm].reshape(tm, hpt, HEAD_DIM)
            k16_ref[...] = kn[:tm].astype(bf16)
            ks_ref[...] = kn[tm:]

        @pl.when((seg == 2) & (j % parts == part))
        def _():
            acc = project()
            v_ref[:, heads, :] = acc[:tm].reshape(tm, hpt, HEAD_DIM)
            v16_ref[...] = acc[:tm].astype(bf16)
            vs_ref[...] = acc[tm:]

    @pl.when(seg == 3)
    def _():
        gate = _sigmoid(project())
        gate_ref[...] = gate[:tm]
        gs_ref[...] = gate[tm:]


def _fox_proj(x, xs, g, w_t, layer, q_gain, k_gain, k_all, v_all, n_slots, *, tm, q_scale, qs_scale, tn=512):
    m, d = x.shape
    ms = xs.shape[0]
    assert ms <= SAMPLE_ROWS
    nseg = d // tn

    def window(seg):
        return lambda i, j: (i, jnp.clip(j - seg * nseg, 0, nseg - 1))

    tiles_per_blk = SUBLANES * HEAD_DIM // tn
    assert nseg % tiles_per_blk == 0

    def slot_window(seg):
        return lambda i, j: (layer, i, jnp.clip(j - seg * nseg, 0, nseg - 1) // tiles_per_blk, 0)

    def sample_window(seg):
        return lambda i, j: (0, jnp.where(i == 0, jnp.clip(j - seg * nseg, 0, nseg - 1), nseg))

    in_specs = [pl.BlockSpec((tm, d), lambda i, j: (i, 0), pipeline_mode=pl.Buffered(1)),
                pl.BlockSpec((ms, d), lambda i, j: (0, 0)),
                pl.BlockSpec((1, d), lambda i, j: (0, 0)), pl.BlockSpec((1, tn, d), lambda i, j: (layer, j, 0)),
                pl.BlockSpec((1, HEAD_DIM), lambda i, j: (0, 0)), pl.BlockSpec((1, HEAD_DIM), lambda i, j: (0, 0))]
    args = [x, xs, g.reshape(1, d), w_t, q_gain.reshape(1, HEAD_DIM), k_gain.reshape(1, HEAD_DIM)]
    aliases = {}
    for buf, out_idx in ((k_all, 1), (v_all, 3)):
        if buf is not None:
            in_specs.append(pl.BlockSpec(memory_space=pl.ANY))
            args.append(buf)
            aliases[len(args) - 1] = out_idx
    blk = lambda seg: pl.BlockSpec((tm, tn), window(seg))
    slot_blk = lambda seg: pl.BlockSpec((None, tm, SUBLANES, HEAD_DIM), slot_window(seg))
    sample_blk = lambda seg: pl.BlockSpec((SAMPLE_ROWS, tn), sample_window(seg))
    full = lambda dt: jax.ShapeDtypeStruct((m, d), dt)
    slots = jax.ShapeDtypeStruct((n_slots, m, d // HEAD_DIM, HEAD_DIM), f32)
    sample = lambda dt: jax.ShapeDtypeStruct((SAMPLE_ROWS, d + tn), dt)
    q, k_all, k16, v_all, v16, gate, q_s, k_s, v_s, gate_s = pl.pallas_call(
        functools.partial(_fox_proj_kernel, tn=tn, nseg=nseg, q_scale=q_scale, qs_scale=qs_scale),
        grid=(m // tm, 4 * nseg),
        in_specs=in_specs,
        out_specs=[blk(0), slot_blk(1), blk(1), slot_blk(2), blk(2), blk(3),
                   sample_blk(0), sample_blk(1), sample_blk(2), sample_blk(3)],
        out_shape=[full(bf16), slots, full(bf16), slots, full(bf16), full(f32),
                   sample(bf16), sample(f32), sample(f32), sample(f32)],
        input_output_aliases=aliases,
        scratch_shapes=[pltpu.VMEM((tm + SAMPLE_ROWS, d), bf16)],
        compiler_params=_params("arbitrary", "arbitrary"),
        name="fox_proj",
    )(*args)
    cut = lambda a: a[:ms, :d]
    return q, k_all, k16, v_all, v16, gate, cut(q_s), cut(k_s), cut(v_s), cut(gate_s)


def _logf_kernel(x_ref, g_ref, w_ref, b_ref, o_ref):
    xn = _rms_rows(x_ref[...], g_ref[...]).astype(bf16)
    z = lax.dot_general(xn, w_ref[...].astype(bf16), (((1,), (1,)), ((), ())),
                        preferred_element_type=f32) + b_ref[...]
    o_ref[...] = jnp.minimum(z, 0.0) - jnp.log1p(jnp.exp(-jnp.abs(z)))


def _logf_proj(x, g, w_ft, b_f, *, tm):
    m, d = x.shape
    return pl.pallas_call(
        _logf_kernel,
        grid=(m // tm,),
        in_specs=[
            pl.BlockSpec((tm, d), lambda i: (i, 0)),
            pl.BlockSpec((1, d), lambda i: (0, 0)),
            pl.BlockSpec((N_HEADS, d), lambda i: (0, 0)),
            pl.BlockSpec((1, N_HEADS), lambda i: (0, 0)),
        ],
        out_specs=pl.BlockSpec((tm, N_HEADS), lambda i: (i, 0)),
        out_shape=jax.ShapeDtypeStruct((m, N_HEADS), f32),
        compiler_params=_params("parallel"),
        name="logf_proj",
    )(x, g.reshape(1, d), w_ft, b_f.reshape(1, N_HEADS))


def _layer_norm_rows(v, g, b):
    mu = jnp.mean(v, axis=-1, keepdims=True)
    vc = v - mu
    var = jnp.mean(vc * vc, axis=-1, keepdims=True)
    return vc * lax.rsqrt(var + LN_EPS) * g + b


def _gmlp_mix_kernel(u_ref, v_ref, g_ref, b_ref, ws_ref, bs_ref, o_ref, *, n_groups, n_chunks):
    vn = _layer_norm_rows(v_ref[...].astype(f32), g_ref[...], b_ref[...]).astype(bf16)
    row = lax.broadcasted_iota(jnp.int32, (CHUNK, CHUNK), 0)
    col = lax.broadcasted_iota(jnp.int32, (CHUNK, CHUNK), 1)
    causal = row >= col
    for gi in range(n_groups):
        cols = slice(gi * GMLP_GROUP, (gi + 1) * GMLP_GROUP)
        ws = jnp.where(causal, ws_ref[gi], 0.0).astype(bf16)
        rhs = jnp.concatenate([vn[c * CHUNK:(c + 1) * CHUNK, cols] for c in range(n_chunks)], axis=1)
        mixed = jnp.dot(ws, rhs, preferred_element_type=f32) + bs_ref[:, gi:gi + 1]
        for c in range(n_chunks):
            rows = slice(c * CHUNK, (c + 1) * CHUNK)
            u = u_ref[rows, cols].astype(f32)
            o_ref[rows, cols] = (u * mixed[:, c * GMLP_GROUP:(c + 1) * GMLP_GROUP]).astype(bf16)


def _gmlp_mix(z, ln_g, ln_b, w_s, b_s_t, *, n_chunks=2):
    m = z.shape[0]
    half = z.shape[1] // 2
    n_groups = half // GMLP_GROUP
    tm = n_chunks * CHUNK
    return pl.pallas_call(
        functools.partial(_gmlp_mix_kernel, n_groups=n_groups, n_chunks=n_chunks),
        grid=(m // tm,),
        in_specs=[
            pl.BlockSpec((tm, half), lambda i: (i, 0)),
            pl.BlockSpec((tm, half), lambda i: (i, 1)),
            pl.BlockSpec((1, half), lambda i: (0, 0)),
            pl.BlockSpec((1, half), lambda i: (0, 0)),
            pl.BlockSpec((n_groups, CHUNK, CHUNK), lambda i: (0, 0, 0)),
            pl.BlockSpec((CHUNK, n_groups), lambda i: (0, 0)),
        ],
        out_specs=pl.BlockSpec((tm, half), lambda i: (i, 0)),
        out_shape=jax.ShapeDtypeStruct((m, half), bf16),
        compiler_params=_params("parallel"),
        name="gmlp_mix",
    )(z, z, ln_g.reshape(1, half), ln_b.reshape(1, half), w_s, b_s_t)


def _gmlp_mix_first_row_kernel(u_ref, v_ref, g_ref, b_ref, w0_ref, b0_ref, o_ref, vn_ref):
    vn = _layer_norm_rows(v_ref[...], g_ref[...], b_ref[...])
    vn_ref[...] = vn
    mixed = vn * w0_ref[...] + b0_ref[...]
    o_ref[...] = (u_ref[...] * mixed).astype(bf16)


def _gmlp_mix_first_row(z, ln_g, ln_b, w00, b0):
    m = z.shape[0]
    half = z.shape[1] // 2
    row = lambda: pl.BlockSpec((1, half), lambda i: (0, 0))
    return pl.pallas_call(
        _gmlp_mix_first_row_kernel,
        grid=(1,),
        in_specs=[
            pl.BlockSpec((m, half), lambda i: (0, 0)),
            pl.BlockSpec((m, half), lambda i: (0, 1)),
            row(), row(), row(), row(),
        ],
        out_specs=[pl.BlockSpec((m, half), lambda i: (0, 0)), pl.BlockSpec((m, half), lambda i: (0, 0))],
        out_shape=[jax.ShapeDtypeStruct((m, half), bf16), jax.ShapeDtypeStruct((m, half), f32)],
        compiler_params=_params("arbitrary"),
        name="gmlp_mix_first_row",
    )(z, z, ln_g.reshape(1, half), ln_b.reshape(1, half), w00.reshape(1, half), b0.reshape(1, half))


def _cumsum_kernel(x_ref, o_ref, *, n_blocks):
    row = lax.broadcasted_iota(jnp.int32, (CHUNK, CHUNK), 0)
    col = lax.broadcasted_iota(jnp.int32, (CHUNK, CHUNK), 1)
    tri = (row >= col).astype(f32)
    carry = jnp.zeros((1, N_HEADS), f32)
    for blk in range(n_blocks):
        rows = slice(blk * CHUNK, (blk + 1) * CHUNK)
        cs = jnp.dot(tri, x_ref[0, rows, :], precision=HIGHEST, preferred_element_type=f32) + carry
        o_ref[0, rows, :] = cs
        carry = cs[CHUNK - 1:CHUNK, :]


def _cumsum_seq(lf):
    b, l, h = lf.shape
    return pl.pallas_call(
        functools.partial(_cumsum_kernel, n_blocks=l // CHUNK),
        grid=(b,),
        in_specs=[pl.BlockSpec((1, l, h), lambda i: (i, 0, 0))],
        out_specs=pl.BlockSpec((1, l, h), lambda i: (i, 0, 0)),
        out_shape=jax.ShapeDtypeStruct((b, l, h), f32),
        compiler_params=_params("parallel"),
        name="cumsum_seq",
    )(lf)


def _split3(x):
    hi = x.astype(bf16).astype(f32)
    r = x - hi
    mid = r.astype(bf16).astype(f32)
    lo = (r - mid).astype(bf16).astype(f32)
    return hi, mid, lo


def _flash_kernel(q_ref, k_ref, v_ref, c_ref, gate_ref, o_ref, kaug_ref, vaug_ref, qside_ref, s_ref, *, tq):
    h = pl.program_id(1)
    qi = pl.program_id(2)
    seq = k_ref.shape[1]

    @pl.when(qi == 0)
    def _():
        head = lax.broadcasted_iota(jnp.int32, (seq, N_HEADS), 1)
        c2 = jnp.sum(jnp.where(head == h, c_ref[0], 0.0), axis=1, keepdims=True) * LOG2E
        hi, mid, lo = _split3(c2)
        lane = lax.broadcasted_iota(jnp.int32, (seq, HEAD_DIM), 1)
        qside = jnp.where(lane == 0, hi, jnp.where(lane == 1, mid, jnp.where(lane == 2, lo,
                          jnp.where(lane < 6, 1.0, 0.0))))
        kside = jnp.where(lane < 3, 1.0, jnp.where(lane == 3, -hi, jnp.where(lane == 4, -mid,
                          jnp.where(lane == 5, -lo, 0.0))))
        qside_ref[...] = qside.astype(bf16)
        kaug_ref[:, :HEAD_DIM] = k_ref[0]
        kaug_ref[:, HEAD_DIM:] = kside.astype(bf16)
        vaug_ref[:, :HEAD_DIM] = v_ref[0]
        vaug_ref[:, HEAD_DIM:] = jnp.where(lane == 0, 1.0, 0.0).astype(bf16)

    q = jnp.concatenate([q_ref[0], qside_ref[pl.ds(pl.multiple_of(qi * tq, tq), tq), :]], axis=1)
    col_minus_row = (lax.broadcasted_iota(jnp.int32, (tq, tq), 1)
                     - lax.broadcasted_iota(jnp.int32, (tq, tq), 0))

    def key_rows(kj):
        return pl.ds(pl.multiple_of(jnp.minimum(kj, qi) * tq, tq), tq)

    def scores(kj):
        return lax.dot_general(q, kaug_ref[key_rows(kj), :], (((1,), (1,)), ((), ())),
                               preferred_element_type=f32)

    def absorb(kj, s, m, acc):
        m_new = jnp.maximum(m, jnp.max(s, axis=1, keepdims=True))
        p = jnp.exp2(s - m_new).astype(bf16)
        pv = jnp.dot(p, vaug_ref[key_rows(kj), :], preferred_element_type=f32)
        return m_new, jnp.exp2(m - m_new) * acc + pv

    s_ref[...] = scores(0)

    def pair(j, carry):
        m, acc = carry
        s_even = s_ref[...]
        s_odd = jnp.where(col_minus_row <= (qi - (2 * j + 1)) * tq, scores(2 * j + 1), -jnp.inf)
        m, acc = absorb(2 * j, s_even, m, acc)
        s_ref[...] = scores(2 * j + 2)
        return absorb(2 * j + 1, s_odd, m, acc)

    init = (jnp.full((tq, 1), -jnp.inf, f32), jnp.zeros((tq, 2 * HEAD_DIM), f32))
    m, acc = lax.fori_loop(0, (qi + 1) // 2, pair, init)

    def finish(acc):
        o = acc[:, :HEAD_DIM] / acc[:, HEAD_DIM:HEAD_DIM + 1]
        o_ref[0] = (o * gate_ref[0]).astype(bf16)

    @pl.when(qi % 2 == 1)
    def _():
        finish(acc)

    @pl.when(qi % 2 == 0)
    def _():
        s = jnp.where(col_minus_row <= 0, s_ref[...], -jnp.inf)
        finish(absorb(qi, s, m, acc)[1])


def _flash_prompt(q, k, v, c, gate, *, tq=FLASH_TILE):
    b, l, d = q.shape
    blk = lambda: pl.BlockSpec((1, tq, HEAD_DIM), lambda bi, h, qi: (bi, qi, h))
    seq = lambda: pl.BlockSpec((1, l, HEAD_DIM), lambda bi, h, qi: (bi, 0, h))
    return pl.pallas_call(
        functools.partial(_flash_kernel, tq=tq),
        grid=(b, N_HEADS, l // tq),
        in_specs=[blk(), seq(), seq(), pl.BlockSpec((1, l, N_HEADS), lambda bi, h, qi: (bi, 0, 0)), blk()],
        out_specs=blk(),
        out_shape=jax.ShapeDtypeStruct((b, l, d), bf16),
        scratch_shapes=[pltpu.VMEM((l, 2 * HEAD_DIM), bf16), pltpu.VMEM((l, 2 * HEAD_DIM), bf16),
                        pltpu.VMEM((l, HEAD_DIM), bf16), pltpu.VMEM((tq, tq), f32)],
        compiler_params=_params("parallel", "parallel", "arbitrary"),
        name="flash_prompt",
    )(q, k, v, c, gate)


T_PER_ROW = LANES // N_HEADS
SCORE_PAGES_PER_STEP = 16
PV_PAGES_PER_STEP = 16


def _eye(n, m):
    return lax.broadcasted_iota(jnp.int32, (n, m), 0) == lax.broadcasted_iota(jnp.int32, (n, m), 1)


def _lane_allreduce(x, op):
    sh = N_HEADS
    while sh < LANES:
        x = op(x, pltpu.roll(x, sh, 1))
        sh *= 2
    return x


def _flat_suffix(x):
    rows = x.shape[0]
    lane = lax.broadcasted_iota(jnp.int32, x.shape, 1)
    row = lax.broadcasted_iota(jnp.int32, x.shape, 0)
    y = x
    sh = N_HEADS
    while sh < LANES:
        y = y + jnp.where(lane < LANES - sh, pltpu.roll(y, LANES - sh, 1), 0.0)
        sh *= 2
    z = _lane_allreduce(jnp.where(lane < N_HEADS, y, 0.0), jnp.add)
    w = z
    sh = 1
    while sh < rows:
        w = w + jnp.where(row < rows - sh, pltpu.roll(w, rows - sh, 0), 0.0)
        sh *= 2
    return (y - x) + (w - z), w[0:1, :]


def _flat_scores(k3, q_rep):
    t = k3.shape[0]
    k2 = k3.reshape(t * N_HEADS, HEAD_DIM).astype(bf16)
    s = lax.dot_general(k2, q_rep, (((1,), (1,)), ((), ())), preferred_element_type=f32)
    s3 = s.reshape(t * N_HEADS // LANES, LANES, LANES)
    return jnp.sum(jnp.where(_eye(LANES, LANES)[None], s3, 0.0), axis=1)


def _dec_scores_kernel(pt_ref, q_ref, *refs, n_steps):
    g_pages = SCORE_PAGES_PER_STEP
    k_refs, lf_refs = refs[:g_pages], refs[g_pages:2 * g_pages]
    knew_ref, lfnew_ref, s_ref, snew_ref, carry_ref = refs[2 * g_pages:]
    p = pl.program_id(1)

    @pl.when(p == 0)
    def _():
        carry_ref[...] = jnp.zeros_like(carry_ref)

    q_rep = q_ref[0]
    carry = carry_ref[...]
    for g in range(g_pages):
        later, total = _flat_suffix(lf_refs[g][0, 0])
        s_ref[0, g_pages - 1 - g] = _flat_scores(k_refs[g][0, 0], q_rep) + (later + carry + lfnew_ref[0])
        carry = carry + total
    carry_ref[...] = carry

    @pl.when(p == n_steps - 1)
    def _():
        s_new = lax.dot_general(knew_ref[0].astype(bf16), q_rep, (((1,), (1,)), ((), ())),
                                preferred_element_type=f32)
        s_new = jnp.sum(jnp.where(_eye(N_HEADS, LANES), s_new, 0.0), axis=0, keepdims=True)
        row = lax.broadcasted_iota(jnp.int32, snew_ref.shape[1:], 0)
        lane = lax.broadcasted_iota(jnp.int32, snew_ref.shape[1:], 1)
        snew_ref[0] = jnp.where((row == 0) & (lane < N_HEADS), s_new, -jnp.inf)


def _dec_scores(page_table, q_rep, cache_k, lf_flat, layer, k_new, lf_new_rep):
    nb, n_pages = page_table.shape
    page = cache_k.shape[2]
    g_pages = SCORE_PAGES_PER_STEP
    n_steps = n_pages // g_pages
    rows = page // T_PER_ROW

    def past(g):
        return lambda b, p, pt: pt[b, n_pages - 1 - (p * g_pages + g)]

    k_specs = [pl.BlockSpec((1, 1, page, N_HEADS, HEAD_DIM),
                            lambda b, p, pt, f=past(g): (layer, f(b, p, pt), 0, 0, 0)) for g in range(g_pages)]
    lf_specs = [pl.BlockSpec((1, 1, rows, LANES),
                             lambda b, p, pt, f=past(g): (layer, f(b, p, pt), 0, 0)) for g in range(g_pages)]
    return pl.pallas_call(
        functools.partial(_dec_scores_kernel, n_steps=n_steps),
        grid_spec=pltpu.PrefetchScalarGridSpec(
            num_scalar_prefetch=1,
            grid=(nb, n_steps),
            in_specs=[pl.BlockSpec((1, LANES, HEAD_DIM), lambda b, p, pt: (b, 0, 0))] + k_specs + lf_specs + [
                pl.BlockSpec((1, N_HEADS, HEAD_DIM), lambda b, p, pt: (b, 0, 0)),
                pl.BlockSpec((1, 1, LANES), lambda b, p, pt: (b, 0, 0)),
            ],
            out_specs=[
                pl.BlockSpec((1, g_pages, rows, LANES), lambda b, p, pt: (b, n_steps - 1 - p, 0, 0)),
                pl.BlockSpec((1, SUBLANES, LANES), lambda b, p, pt: (b, 0, 0)),
            ],
            scratch_shapes=[pltpu.VMEM((1, LANES), f32)],
        ),
        out_shape=[jax.ShapeDtypeStruct((nb, n_pages, rows, LANES), f32),
                   jax.ShapeDtypeStruct((nb, SUBLANES, LANES), f32)],
        compiler_params=_params("parallel", "arbitrary"),
        name="dec_scores",
    )(page_table, q_rep, *([cache_k] * g_pages), *([lf_flat] * g_pages), k_new, lf_new_rep)


def _lane_broadcast_rows(flat):
    r = flat.shape[0]
    spread = jnp.where(_eye(LANES, LANES)[None], jnp.broadcast_to(flat[:, None, :], (r, LANES, LANES)), 0.0)
    ones = jnp.ones((LANES, HEAD_DIM), bf16)
    return jnp.dot(spread.reshape(r * LANES, LANES).astype(bf16), ones, preferred_element_type=f32)


def _dec_pv_kernel(pt_ref, s_ref, snew_ref, *refs, n_steps):
    g_pages = PV_PAGES_PER_STEP
    v_refs = refs[:g_pages]
    vnew_ref, gate_ref, o_ref, p_ref, pnew_ref, acc_ref = refs[g_pages:]
    p = pl.program_id(1)

    @pl.when(p == 0)
    def _():
        s = s_ref[0]
        s_new = snew_ref[0]
        m = jnp.maximum(jnp.max(jnp.max(s, axis=0), axis=0, keepdims=True), jnp.max(s_new, axis=0, keepdims=True))
        m = _lane_allreduce(m, jnp.maximum)
        e = jnp.exp(s - m[None])
        e_new = jnp.exp(s_new - m)
        denom = jnp.sum(jnp.sum(e, axis=0), axis=0, keepdims=True) + jnp.sum(e_new, axis=0, keepdims=True)
        denom = _lane_allreduce(denom, jnp.add)
        p_ref[...] = e / denom[None]
        pnew_ref[...] = e_new / denom
        acc_ref[...] = jnp.zeros_like(acc_ref)

    acc = acc_ref[...]
    for g in range(g_pages):
        v3 = v_refs[g][0, 0]
        pb = _lane_broadcast_rows(p_ref[p * g_pages + g])
        acc = acc + jnp.sum(pb.reshape(v3.shape) * v3, axis=0)
    acc_ref[...] = acc

    @pl.when(p == n_steps - 1)
    def _():
        p_new = jnp.where(_eye(N_HEADS, LANES), jnp.broadcast_to(pnew_ref[0:1, :], (N_HEADS, LANES)), 0.0)
        pb = jnp.dot(p_new.astype(bf16), jnp.ones((LANES, HEAD_DIM), bf16), preferred_element_type=f32)
        o_ref[0] = (acc + pb * vnew_ref[0]) * gate_ref[0]


def _dec_pv(page_table, s_all, s_new, cache_v, layer, v_new, gate):
    nb, n_pages = page_table.shape
    page = cache_v.shape[2]
    g_pages = PV_PAGES_PER_STEP
    n_steps = n_pages // g_pages
    rows = page // T_PER_ROW
    head_blk = lambda: pl.BlockSpec((1, N_HEADS, HEAD_DIM), lambda b, p, pt: (b, 0, 0))
    v_specs = [pl.BlockSpec((1, 1, page, N_HEADS, HEAD_DIM),
                            lambda b, p, pt, g=g: (layer, pt[b, p * g_pages + g], 0, 0, 0)) for g in range(g_pages)]
    return pl.pallas_call(
        functools.partial(_dec_pv_kernel, n_steps=n_steps),
        grid_spec=pltpu.PrefetchScalarGridSpec(
            num_scalar_prefetch=1,
            grid=(nb, n_steps),
            in_specs=[
                pl.BlockSpec((1, n_pages, rows, LANES), lambda b, p, pt: (b, 0, 0, 0)),
                pl.BlockSpec((1, SUBLANES, LANES), lambda b, p, pt: (b, 0, 0)),
            ] + v_specs + [head_blk(), head_blk()],
            out_specs=head_blk(),
            scratch_shapes=[pltpu.VMEM((n_pages, rows, LANES), f32), pltpu.VMEM((SUBLANES, LANES), f32),
                            pltpu.VMEM((N_HEADS, HEAD_DIM), f32)],
        ),
        out_shape=jax.ShapeDtypeStruct((nb, N_HEADS, HEAD_DIM), f32),
        compiler_params=_params("parallel", "arbitrary"),
        name="dec_pv",
    )(page_table, s_all, s_new, *([cache_v] * g_pages), v_new, gate)


def _final_norm_kernel(x_ref, g_ref, o_ref):
    o_ref[...] = _rms_rows(x_ref[...], g_ref[...])


def _final_norm(x, g, *, tm):
    m, d = x.shape
    return pl.pallas_call(
        _final_norm_kernel,
        grid=(m // tm,),
        in_specs=[pl.BlockSpec((tm, d), lambda i: (i, 0)), pl.BlockSpec((1, d), lambda i: (0, 0))],
        out_specs=pl.BlockSpec((tm, d), lambda i: (i, 0)),
        out_shape=jax.ShapeDtypeStruct((m, d), f32),
        compiler_params=_params("parallel"),
        name="final_norm",
    )(x, g.reshape(1, d))


def kernel(x_prompt, x_sample, cache_k, cache_v, cache_logf, page_table, norm_ffn1, ffn1_w_up, ffn1_w_down, norm_mix, norm_ffn2, ffn2_w_up, ffn2_w_down, gm_w_in, gm_ln_g, gm_ln_b, gm_w_s, gm_b_s, gm_w_out, fox_w_in, fox_b_f, fox_q_norm, fox_k_norm, fox_w_out, norm_final):
    bp, lp, d = x_prompt.shape
    bs, ls, _ = x_sample.shape
    assert ls == 1 and lp % FLASH_TILE == 0 and d == N_HEADS * HEAD_DIM
    depth = norm_ffn1.shape[0]
    mp, ms = bp * lp, bs * ls
    tm_p = min(ROW_TILE, mp)
    tm_down = min(ROW_TILE // 2, mp)
    half = gm_w_in.shape[2] // 2

    xp = x_prompt.reshape(mp, d)
    xs = x_sample.reshape(ms, d)
    lf_p, k_s, v_s, lf_s, gv_s = [], [], [], [], []
    kp_all = vp_all = None
    n_fox = fox_w_in.shape[0]
    n_pool, page = cache_logf.shape[1:3]
    assert page == LANES and page_table.shape[1] % max(SCORE_PAGES_PER_STEP, PV_PAGES_PER_STEP) == 0
    fox_w_t = jnp.swapaxes(fox_w_in, 1, 2)
    lf_flat = cache_logf.reshape(n_fox, n_pool, page // T_PER_ROW, LANES)

    for i in range(depth):
        xp, xs = _ffn(xp, xs, norm_ffn1[i], ffn1_w_up, ffn1_w_down, i, tm_up=tm_p, tm_down=tm_down)
        j = i // 2
        if i % 2 == 0:
            zp, zs = _gelu_proj(xp, xs, norm_mix[i], gm_w_in, j, tm=tm_p)
            gp = _gmlp_mix(zp, gm_ln_g[j], gm_ln_b[j], gm_w_s[j], gm_b_s[j].T)
            w00 = jnp.repeat(gm_w_s[j, :, 0, 0], GMLP_GROUP)
            b0 = jnp.repeat(gm_b_s[j, :, 0], GMLP_GROUP)
            gs, gv = _gmlp_mix_first_row(zs, gm_ln_g[j], gm_ln_b[j], w00, b0)
            gv_s.append(gv.reshape(bs, ls, half))
            xp, xs = _mm_res(gp, gs, gm_w_out, j, xp, xs, 1.0, tm=tm_p)
        else:
            w_ft = fox_w_t[j, 4 * d:, :]
            qp, kp_all, kp16, vp_all, vp16, gatep, qs, ks, vs, gates = _fox_proj(
                xp, xs, norm_mix[i], fox_w_t, j, fox_q_norm[j], fox_k_norm[j], kp_all, vp_all, n_fox,
                tm=tm_p, q_scale=SCORE_SCALE * LOG2E, qs_scale=SCORE_SCALE)
            lfp = _logf_proj(xp, norm_mix[i], w_ft, fox_b_f[j], tm=tm_p).reshape(bp, lp, N_HEADS)
            og = _flash_prompt(qp.reshape(bp, lp, d), kp16.reshape(bp, lp, d), vp16.reshape(bp, lp, d),
                               _cumsum_seq(lfp), gatep.reshape(bp, lp, d))
            lf_p.append(lfp)
            lfs = _logf_proj(xs, norm_mix[i], w_ft, fox_b_f[j], tm=ms)
            q_rep = jnp.tile(qs.reshape(bs, N_HEADS, HEAD_DIM), (1, T_PER_ROW, 1))
            lf_new_rep = jnp.tile(lfs, (1, T_PER_ROW)).reshape(bs, 1, LANES)
            ks3 = ks.reshape(bs, N_HEADS, HEAD_DIM)
            vs3 = vs.reshape(bs, N_HEADS, HEAD_DIM)
            k_s.append(ks3.reshape(bs, ls, N_HEADS, HEAD_DIM))
            v_s.append(vs3.reshape(bs, ls, N_HEADS, HEAD_DIM))
            s_all, s_new = _dec_scores(page_table, q_rep, cache_k, lf_flat, j, ks3, lf_new_rep)
            os_ = _dec_pv(page_table, s_all, s_new, cache_v, j, vs3, gates.reshape(bs, N_HEADS, HEAD_DIM))
            lf_s.append(lfs.reshape(bs, ls, N_HEADS))
            xp, xs = _mm_res(og.reshape(mp, d), os_.reshape(ms, d).astype(bf16), fox_w_out, j, xp, xs, 1.0, tm=tm_p)
        xp, xs = _ffn(xp, xs, norm_ffn2[i], ffn2_w_up, ffn2_w_down, i, tm_up=tm_p, tm_down=tm_down)

    y_prompt = _final_norm(xp, norm_final, tm=tm_down).reshape(bp, lp, d)
    y_sample = _final_norm(xs, norm_final, tm=ms).reshape(bs, ls, d)
    heads_p = (n_fox, bp, lp, N_HEADS, HEAD_DIM)
    return (y_prompt, y_sample, kp_all.reshape(heads_p), vp_all.reshape(heads_p), jnp.stack(lf_p),
            jnp.stack(k_s), jnp.stack(v_s), jnp.stack(lf_s), jnp.stack(gv_s))
```

```python
import functools

import jax
import jax.numpy as jnp
from jax import lax
from jax.experimental import pallas as pl
from jax.experimental.pallas import tpu as pltpu

f32 = jnp.float32
bf16 = jnp.bfloat16

N_HEADS = 16
HEAD_DIM = 128
CHUNK = 128
GMLP_GROUP = 128
NORM_EPS = 1e-6
LN_EPS = 1e-5
FFN_RESIDUAL = 0.5
LANES = 128
SUBLANES = 8
MXU_COLS = 256
ROW_TILE = 1024
FLASH_TILE = 512

VMEM_LIMIT_BYTES = 56 * 1024 * 1024
HIGHEST = lax.Precision.HIGHEST
LOG2E = 1.4426950408889634
SCORE_SCALE = HEAD_DIM ** -0.5


def _params(*sem):
    return pltpu.CompilerParams(dimension_semantics=sem, vmem_limit_bytes=VMEM_LIMIT_BYTES)


def _rms_rows(x, g):
    ms = jnp.mean(x * x, axis=-1, keepdims=True)
    return x * lax.rsqrt(ms + NORM_EPS) * g


def _sigmoid(x):
    return 1.0 / (1.0 + jnp.exp(-x))


SAMPLE_ROWS = 16


def _ffn_up_kernel(x_ref, xs_ref, g_ref, wa_ref, wb_ref, o_ref, os_ref, xn_ref, *, nb, tn, nvalid, sub):
    j = pl.program_id(1)
    tm = x_ref.shape[0]
    ms = xs_ref.shape[0]

    @pl.when(j == 0)
    def _():
        xn_ref[:tm, :] = _rms_rows(x_ref[...], g_ref[...]).astype(bf16)
        xsn = _rms_rows(xs_ref[...], g_ref[...])
        pad = jnp.zeros((SAMPLE_ROWS - ms, xsn.shape[1]), f32)
        xn_ref[tm:, :] = jnp.concatenate([xsn, pad], axis=0).astype(bf16)

    def emit(ncols, b_off):
        first = pl.program_id(0) == 0
        pl.when(first)(lambda: emit_rows(ncols, b_off, True))
        pl.when(jnp.logical_not(first))(lambda: emit_rows(ncols, b_off, False))

    def emit_rows(ncols, b_off, with_sample):
        xn = xn_ref[...] if with_sample else xn_ref[:tm, :]
        if not with_sample:
            os_ref[...] = jnp.zeros(os_ref.shape, os_ref.dtype)
        for c0 in range(0, ncols, sub):
            c1 = min(c0 + sub, ncols)
            wa = wa_ref[0, :, c0:c1].astype(bf16)
            wb = wb_ref[0, :, b_off + c0:b_off + c1].astype(bf16)
            if 2 * (c1 - c0) <= MXU_COLS:
                ab = jnp.dot(xn, jnp.concatenate([wa, wb], axis=1), preferred_element_type=f32)
                a, b = ab[:, :c1 - c0], ab[:, c1 - c0:]
            else:
                a = jnp.dot(xn, wa, preferred_element_type=f32)
                b = jnp.dot(xn, wb, preferred_element_type=f32)
            h = (a * _sigmoid(a) * b).astype(o_ref.dtype)
            o_ref[:, c0:c1] = h[:tm]
            if with_sample:
                os_ref[:, c0:c1] = h[tm:]
        if with_sample and ncols < tn:
            os_ref[:, ncols:] = jnp.zeros((SAMPLE_ROWS, tn - ncols), os_ref.dtype)

    if nvalid == tn:
        emit(tn, 0)
    else:
        @pl.when(j < nb - 1)
        def _():
            emit(tn, 0)

        @pl.when(j == nb - 1)
        def _():
            emit(nvalid, tn - nvalid)


def _ffn_up(x, xs, g, w_up, layer, *, tm, tn=512, sub=256):
    m, d = x.shape
    ms = xs.shape[0]
    assert ms <= SAMPLE_ROWS
    dff = w_up.shape[2] // 2
    nb = pl.cdiv(dff, tn)
    nvalid = dff - (nb - 1) * tn
    kern = functools.partial(_ffn_up_kernel, nb=nb, tn=tn, nvalid=nvalid, sub=sub)
    h, hs = pl.pallas_call(
        kern,
        grid=(m // tm, nb),
        in_specs=[
            pl.BlockSpec((tm, d), lambda i, j: (i, 0)),
            pl.BlockSpec((ms, d), lambda i, j: (0, 0)),
            pl.BlockSpec((1, d), lambda i, j: (0, 0)),
            pl.BlockSpec((1, d, tn), lambda i, j: (layer, 0, j)),
            pl.BlockSpec((pl.Element(1), pl.Element(d), pl.Element(tn)),
                         lambda i, j: (layer, 0, pl.multiple_of(jnp.minimum(dff + j * tn, 2 * dff - tn), LANES))),
        ],
        out_specs=[pl.BlockSpec((tm, tn), lambda i, j: (i, j)),
                   pl.BlockSpec((SAMPLE_ROWS, tn), lambda i, j: (0, jnp.where(i == 0, j, nb)))],
        out_shape=[jax.ShapeDtypeStruct((m, dff), bf16),
                   jax.ShapeDtypeStruct((SAMPLE_ROWS, (nb + 1) * tn), bf16)],
        scratch_shapes=[pltpu.VMEM((tm + SAMPLE_ROWS, d), bf16)],
        compiler_params=_params("arbitrary", "arbitrary"),
        name="ffn_up",
    )(x, xs, g.reshape(1, d), w_up, w_up)
    return h, hs[:ms, :dff]


def _mm_res_kernel(h_ref, hs_ref, w_ref, r_ref, rs_ref, o_ref, os_ref, wb_ref, *, scale):
    @pl.when(pl.program_id(1) == 0)
    def _():
        wb_ref[...] = w_ref[0].astype(bf16)
        os_ref[...] = rs_ref[...] + scale * jnp.dot(hs_ref[...], wb_ref[...], preferred_element_type=f32)

    acc = jnp.dot(h_ref[...], wb_ref[...], preferred_element_type=f32)
    o_ref[...] = r_ref[...] + scale * acc


def _mm_res(h, hs, w, layer, res, res_s, scale, *, tm, tn=512):
    m, k = h.shape
    ms = hs.shape[0]
    n = w.shape[2]
    return pl.pallas_call(
        functools.partial(_mm_res_kernel, scale=scale),
        grid=(n // tn, m // tm),
        in_specs=[
            pl.BlockSpec((tm, k), lambda j, i: (i, 0)),
            pl.BlockSpec((ms, k), lambda j, i: (0, 0)),
            pl.BlockSpec((1, k, tn), lambda j, i: (layer, 0, j)),
            pl.BlockSpec((tm, tn), lambda j, i: (i, j)),
            pl.BlockSpec((ms, tn), lambda j, i: (0, j)),
        ],
        out_specs=[pl.BlockSpec((tm, tn), lambda j, i: (i, j)), pl.BlockSpec((ms, tn), lambda j, i: (0, j))],
        out_shape=[jax.ShapeDtypeStruct((m, n), f32), jax.ShapeDtypeStruct((ms, n), f32)],
        scratch_shapes=[pltpu.VMEM((k, tn), bf16)],
        compiler_params=_params("parallel", "arbitrary"),
        name="mm_res",
    )(h, hs, w, res, res_s)


def _ffn(x, xs, g, w_up, w_down, layer, *, tm_up, tm_down):
    h, hs = _ffn_up(x, xs, g, w_up, layer, tm=tm_up)
    return _mm_res(h, hs, w_down, layer, x, xs, FFN_RESIDUAL, tm=tm_down)


def _head_rms(acc, gain, tn):
    outs = []
    for hh in range(tn // HEAD_DIM):
        blk = acc[:, hh * HEAD_DIM:(hh + 1) * HEAD_DIM]
        ms = jnp.mean(blk * blk, axis=-1, keepdims=True)
        outs.append(blk * lax.rsqrt(ms + NORM_EPS) * gain)
    return jnp.concatenate(outs, axis=1)


def _gelu_proj_kernel(x_ref, xs_ref, g_ref, w_ref, o_ref, os_ref, xn_ref):
    tm = x_ref.shape[0]
    ms = xs_ref.shape[0]

    @pl.when(pl.program_id(1) == 0)
    def _():
        xn_ref[:tm, :] = _rms_rows(x_ref[...], g_ref[...]).astype(bf16)
        xsn = _rms_rows(xs_ref[...], g_ref[...])
        pad = jnp.zeros((SAMPLE_ROWS - ms, xsn.shape[1]), f32)
        xn_ref[tm:, :] = jnp.concatenate([xsn, pad], axis=0).astype(bf16)

    acc = jnp.dot(xn_ref[...], w_ref[0].astype(bf16), preferred_element_type=f32)
    z = 0.5 * acc * (1.0 + lax.erf(acc * (0.5 ** 0.5)))
    o_ref[...] = z[:tm].astype(o_ref.dtype)
    os_ref[...] = z[tm:]


def _gelu_proj(x, xs, g, w, layer, *, tm, tn=512):
    m, d = x.shape
    ms = xs.shape[0]
    assert ms <= SAMPLE_ROWS
    ncols = w.shape[2]
    nbn = ncols // tn
    out, out_s = pl.pallas_call(
        _gelu_proj_kernel,
        grid=(m // tm, nbn),
        in_specs=[pl.BlockSpec((tm, d), lambda i, j: (i, 0)), pl.BlockSpec((ms, d), lambda i, j: (0, 0)),
                  pl.BlockSpec((1, d), lambda i, j: (0, 0)), pl.BlockSpec((1, d, tn), lambda i, j: (layer, 0, j))],
        out_specs=[pl.BlockSpec((tm, tn), lambda i, j: (i, j)),
                   pl.BlockSpec((SAMPLE_ROWS, tn), lambda i, j: (0, jnp.where(i == 0, j, nbn)))],
        out_shape=[jax.ShapeDtypeStruct((m, ncols), bf16), jax.ShapeDtypeStruct((SAMPLE_ROWS, ncols + tn), f32)],
        scratch_shapes=[pltpu.VMEM((tm + SAMPLE_ROWS, d), bf16)],
        compiler_params=_params("arbitrary", "arbitrary"),
        name="gelu_proj",
    )(x, xs, g.reshape(1, d), w)
    return out, out_s[:ms, :ncols]


def _fox_proj_kernel(x_ref, xs_ref, g_ref, w_ref, qg_ref, kg_ref, *refs, tn, nseg, q_scale, qs_scale):
    q_ref, k_ref, k16_ref, v_ref, v16_ref, gate_ref, qs_ref, ks_ref, vs_ref, gs_ref, xn_ref = refs[-11:]
    j = pl.program_id(1)
    tm = x_ref.shape[0]
    ms = xs_ref.shape[0]

    @pl.when(j == 0)
    def _():
        xn_ref[:tm, :] = _rms_rows(x_ref[...], g_ref[...]).astype(bf16)
        xsn = _rms_rows(xs_ref[...], g_ref[...])
        pad = jnp.zeros((SAMPLE_ROWS - ms, xsn.shape[1]), f32)
        xn_ref[tm:, :] = jnp.concatenate([xsn, pad], axis=0).astype(bf16)

    def project():
        return lax.dot_general(xn_ref[...], w_ref[0].astype(bf16), (((1,), (1,)), ((), ())),
                               preferred_element_type=f32)

    seg = j // nseg

    @pl.when(seg == 0)
    def _():
        qn = _head_rms(project(), qg_ref[...], tn)
        q_ref[...] = (qn[:tm] * q_scale).astype(bf16)
        qs_ref[...] = (qn[tm:] * qs_scale).astype(bf16)

    hpt = tn // HEAD_DIM
    parts = k_ref.shape[1] // hpt
    for part in range(parts):
        heads = slice(part * hpt, (part + 1) * hpt)

        @pl.when((seg == 1) & (j % parts == part))
        def _():
            kn = _head_rms(project(), kg_ref[...], tn)
            k_ref[:, heads, :] = kn[:tm].reshape(tm, hpt, HEAD_DIM)
            k16_ref[...] = kn[:tm].astype(bf16)
            ks_ref[...] = kn[tm:]

        @pl.when((seg == 2) & (j % parts == part))
        def _():
            acc = project()
            v_ref[:, heads, :] = acc[:tm].reshape(tm, hpt, HEAD_DIM)
            v16_ref[...] = acc[:tm].astype(bf16)
            vs_ref[...] = acc[tm:]

    @pl.when(seg == 3)
    def _():
        gate = _sigmoid(project())
        gate_ref[...] = gate[:tm]
        gs_ref[...] = gate[tm:]


def _fox_proj(x, xs, g, w_t, layer, q_gain, k_gain, k_all, v_all, n_slots, *, tm, q_scale, qs_scale, tn=512):
    m, d = x.shape
    ms = xs.shape[0]
    assert ms <= SAMPLE_ROWS
    nseg = d // tn

    def window(seg):
        return lambda i, j: (i, jnp.clip(j - seg * nseg, 0, nseg - 1))

    tiles_per_blk = SUBLANES * HEAD_DIM // tn
    assert nseg % tiles_per_blk == 0

    def slot_window(seg):
        return lambda i, j: (layer, i, jnp.clip(j - seg * nseg, 0, nseg - 1) // tiles_per_blk, 0)

    def sample_window(seg):
        return lambda i, j: (0, jnp.where(i == 0, jnp.clip(j - seg * nseg, 0, nseg - 1), nseg))

    in_specs = [pl.BlockSpec((tm, d), lambda i, j: (i, 0), pipeline_mode=pl.Buffered(1)),
                pl.BlockSpec((ms, d), lambda i, j: (0, 0)),
                pl.BlockSpec((1, d), lambda i, j: (0, 0)), pl.BlockSpec((1, tn, d), lambda i, j: (layer, j, 0)),
                pl.BlockSpec((1, HEAD_DIM), lambda i, j: (0, 0)), pl.BlockSpec((1, HEAD_DIM), lambda i, j: (0, 0))]
    args = [x, xs, g.reshape(1, d), w_t, q_gain.reshape(1, HEAD_DIM), k_gain.reshape(1, HEAD_DIM)]
    aliases = {}
    for buf, out_idx in ((k_all, 1), (v_all, 3)):
        if buf is not None:
            in_specs.append(pl.BlockSpec(memory_space=pl.ANY))
            args.append(buf)
            aliases[len(args) - 1] = out_idx
    blk = lambda seg: pl.BlockSpec((tm, tn), window(seg))
    slot_blk = lambda seg: pl.BlockSpec((None, tm, SUBLANES, HEAD_DIM), slot_window(seg))
    sample_blk = lambda seg: pl.BlockSpec((SAMPLE_ROWS, tn), sample_window(seg))
    full = lambda dt: jax.ShapeDtypeStruct((m, d), dt)
    slots = jax.ShapeDtypeStruct((n_slots, m, d // HEAD_DIM, HEAD_DIM), f32)
    sample = lambda dt: jax.ShapeDtypeStruct((SAMPLE_ROWS, d + tn), dt)
    q, k_all, k16, v_all, v16, gate, q_s, k_s, v_s, gate_s = pl.pallas_call(
        functools.partial(_fox_proj_kernel, tn=tn, nseg=nseg, q_scale=q_scale, qs_scale=qs_scale),
        grid=(m // tm, 4 * nseg),
        in_specs=in_specs,
        out_specs=[blk(0), slot_blk(1), blk(1), slot_blk(2), blk(2), blk(3),
                   sample_blk(0), sample_blk(1), sample_blk(2), sample_blk(3)],
        out_shape=[full(bf16), slots, full(bf16), slots, full(bf16), full(f32),
                   sample(bf16), sample(f32), sample(f32), sample(f32)],
        input_output_aliases=aliases,
        scratch_shapes=[pltpu.VMEM((tm + SAMPLE_ROWS, d), bf16)],
        compiler_params=_params("arbitrary", "arbitrary"),
        name="fox_proj",
    )(*args)
    cut = lambda a: a[:ms, :d]
    return q, k_all, k16, v_all, v16, gate, cut(q_s), cut(k_s), cut(v_s), cut(gate_s)


def _logf_kernel(x_ref, g_ref, w_ref, b_ref, o_ref):
    xn = _rms_rows(x_ref[...], g_ref[...]).astype(bf16)
    z = lax.dot_general(xn, w_ref[...].astype(bf16), (((1,), (1,)), ((), ())),
                        preferred_element_type=f32) + b_ref[...]
    o_ref[...] = jnp.minimum(z, 0.0) - jnp.log1p(jnp.exp(-jnp.abs(z)))


def _logf_proj(x, g, w_ft, b_f, *, tm):
    m, d = x.shape
    return pl.pallas_call(
        _logf_kernel,
        grid=(m // tm,),
        in_specs=[
            pl.BlockSpec((tm, d), lambda i: (i, 0)),
            pl.BlockSpec((1, d), lambda i: (0, 0)),
            pl.BlockSpec((N_HEADS, d), lambda i: (0, 0)),
            pl.BlockSpec((1, N_HEADS), lambda i: (0, 0)),
        ],
        out_specs=pl.BlockSpec((tm, N_HEADS), lambda i: (i, 0)),
        out_shape=jax.ShapeDtypeStruct((m, N_HEADS), f32),
        compiler_params=_params("parallel"),
        name="logf_proj",
    )(x, g.reshape(1, d), w_ft, b_f.reshape(1, N_HEADS))


def _layer_norm_rows(v, g, b):
    mu = jnp.mean(v, axis=-1, keepdims=True)
    vc = v - mu
    var = jnp.mean(vc * vc, axis=-1, keepdims=True)
    return vc * lax.rsqrt(var + LN_EPS) * g + b


def _gmlp_mix_kernel(u_ref, v_ref, g_ref, b_ref, ws_ref, bs_ref, o_ref, *, n_groups, n_chunks):
    vn = _layer_norm_rows(v_ref[...].astype(f32), g_ref[...], b_ref[...]).astype(bf16)
    row = lax.broadcasted_iota(jnp.int32, (CHUNK, CHUNK), 0)
    col = lax.broadcasted_iota(jnp.int32, (CHUNK, CHUNK), 1)
    causal = row >= col
    for gi in range(n_groups):
        cols = slice(gi * GMLP_GROUP, (gi + 1) * GMLP_GROUP)
        ws = jnp.where(causal, ws_ref[gi], 0.0).astype(bf16)
        rhs = jnp.concatenate([vn[c * CHUNK:(c + 1) * CHUNK, cols] for c in range(n_chunks)], axis=1)
        mixed = jnp.dot(ws, rhs, preferred_element_type=f32) + bs_ref[:, gi:gi + 1]
        for c in range(n_chunks):
            rows = slice(c * CHUNK, (c + 1) * CHUNK)
            u = u_ref[rows, cols].astype(f32)
            o_ref[rows, cols] = (u * mixed[:, c * GMLP_GROUP:(c + 1) * GMLP_GROUP]).astype(bf16)


def _gmlp_mix(z, ln_g, ln_b, w_s, b_s_t, *, n_chunks=2):
    m = z.shape[0]
    half = z.shape[1] // 2
    n_groups = half // GMLP_GROUP
    tm = n_chunks * CHUNK
    return pl.pallas_call(
        functools.partial(_gmlp_mix_kernel, n_groups=n_groups, n_chunks=n_chunks),
        grid=(m // tm,),
        in_specs=[
            pl.BlockSpec((tm, half), lambda i: (i, 0)),
            pl.BlockSpec((tm, half), lambda i: (i, 1)),
            pl.BlockSpec((1, half), lambda i: (0, 0)),
            pl.BlockSpec((1, half), lambda i: (0, 0)),
            pl.BlockSpec((n_groups, CHUNK, CHUNK), lambda i: (0, 0, 0)),
            pl.BlockSpec((CHUNK, n_groups), lambda i: (0, 0)),
        ],
        out_specs=pl.BlockSpec((tm, half), lambda i: (i, 0)),
        out_shape=jax.ShapeDtypeStruct((m, half), bf16),
        compiler_params=_params("parallel"),
        name="gmlp_mix",
    )(z, z, ln_g.reshape(1, half), ln_b.reshape(1, half), w_s, b_s_t)


def _gmlp_mix_first_row_kernel(u_ref, v_ref, g_ref, b_ref, w0_ref, b0_ref, o_ref, vn_ref):
    vn = _layer_norm_rows(v_ref[...], g_ref[...], b_ref[...])
    vn_ref[...] = vn
    mixed = vn * w0_ref[...] + b0_ref[...]
    o_ref[...] = (u_ref[...] * mixed).astype(bf16)


def _gmlp_mix_first_row(z, ln_g, ln_b, w00, b0):
    m = z.shape[0]
    half = z.shape[1] // 2
    row = lambda: pl.BlockSpec((1, half), lambda i: (0, 0))
    return pl.pallas_call(
        _gmlp_mix_first_row_kernel,
        grid=(1,),
        in_specs=[
            pl.BlockSpec((m, half), lambda i: (0, 0)),
            pl.BlockSpec((m, half), lambda i: (0, 1)),
            row(), row(), row(), row(),
        ],
        out_specs=[pl.BlockSpec((m, half), lambda i: (0, 0)), pl.BlockSpec((m, half), lambda i: (0, 0))],
        out_shape=[jax.ShapeDtypeStruct((m, half), bf16), jax.ShapeDtypeStruct((m, half), f32)],
        compiler_params=_params("arbitrary"),
        name="gmlp_mix_first_row",
    )(z, z, ln_g.reshape(1, half), ln_b.reshape(1, half), w00.reshape(1, half), b0.reshape(1, half))


def _cumsum_kernel(x_ref, o_ref, *, n_blocks):
    row = lax.broadcasted_iota(jnp.int32, (CHUNK, CHUNK), 0)
    col = lax.broadcasted_iota(jnp.int32, (CHUNK, CHUNK), 1)
    tri = (row >= col).astype(f32)
    carry = jnp.zeros((1, N_HEADS), f32)
    for blk in range(n_blocks):
        rows = slice(blk * CHUNK, (blk + 1) * CHUNK)
        cs = jnp.dot(tri, x_ref[0, rows, :], precision=HIGHEST, preferred_element_type=f32) + carry
        o_ref[0, rows, :] = cs
        carry = cs[CHUNK - 1:CHUNK, :]


def _cumsum_seq(lf):
    b, l, h = lf.shape
    return pl.pallas_call(
        functools.partial(_cumsum_kernel, n_blocks=l // CHUNK),
        grid=(b,),
        in_specs=[pl.BlockSpec((1, l, h), lambda i: (i, 0, 0))],
        out_specs=pl.BlockSpec((1, l, h), lambda i: (i, 0, 0)),
        out_shape=jax.ShapeDtypeStruct((b, l, h), f32),
        compiler_params=_params("parallel"),
        name="cumsum_seq",
    )(lf)


def _split3(x):
    hi = x.astype(bf16).astype(f32)
    r = x - hi
    mid = r.astype(bf16).astype(f32)
    lo = (r - mid).astype(bf16).astype(f32)
    return hi, mid, lo


def _flash_kernel(q_ref, k_ref, v_ref, c_ref, gate_ref, o_ref, kaug_ref, vaug_ref, qside_ref, s_ref, *, tq):
    h = pl.program_id(1)
    qi = pl.program_id(2)
    seq = k_ref.shape[1]

    @pl.when(qi == 0)
    def _():
        head = lax.broadcasted_iota(jnp.int32, (seq, N_HEADS), 1)
        c2 = jnp.sum(jnp.where(head == h, c_ref[0], 0.0), axis=1, keepdims=True) * LOG2E
        hi, mid, lo = _split3(c2)
        lane = lax.broadcasted_iota(jnp.int32, (seq, HEAD_DIM), 1)
        qside = jnp.where(lane == 0, hi, jnp.where(lane == 1, mid, jnp.where(lane == 2, lo,
                          jnp.where(lane < 6, 1.0, 0.0))))
        kside = jnp.where(lane < 3, 1.0, jnp.where(lane == 3, -hi, jnp.where(lane == 4, -mid,
                          jnp.where(lane == 5, -lo, 0.0))))
        qside_ref[...] = qside.astype(bf16)
        kaug_ref[:, :HEAD_DIM] = k_ref[0]
        kaug_ref[:, HEAD_DIM:] = kside.astype(bf16)
        vaug_ref[:, :HEAD_DIM] = v_ref[0]
        vaug_ref[:, HEAD_DIM:] = jnp.where(lane == 0, 1.0, 0.0).astype(bf16)

    q = jnp.concatenate([q_ref[0], qside_ref[pl.ds(pl.multiple_of(qi * tq, tq), tq), :]], axis=1)
    col_minus_row = (lax.broadcasted_iota(jnp.int32, (tq, tq), 1)
                     - lax.broadcasted_iota(jnp.int32, (tq, tq), 0))

    def key_rows(kj):
        return pl.ds(pl.multiple_of(jnp.minimum(kj, qi) * tq, tq), tq)

    def scores(kj):
        return lax.dot_general(q, kaug_ref[key_rows(kj), :], (((1,), (1,)), ((), ())),
                               preferred_element_type=f32)

    def absorb(kj, s, m, acc):
        m_new = jnp.maximum(m, jnp.max(s, axis=1, keepdims=True))
        p = jnp.exp2(s - m_new).astype(bf16)
        pv = jnp.dot(p, vaug_ref[key_rows(kj), :], preferred_element_type=f32)
        return m_new, jnp.exp2(m - m_new) * acc + pv

    s_ref[...] = scores(0)

    def pair(j, carry):
        m, acc = carry
        s_even = s_ref[...]
        s_odd = jnp.where(col_minus_row <= (qi - (2 * j + 1)) * tq, scores(2 * j + 1), -jnp.inf)
        m, acc = absorb(2 * j, s_even, m, acc)
        s_ref[...] = scores(2 * j + 2)
        return absorb(2 * j + 1, s_odd, m, acc)

    init = (jnp.full((tq, 1), -jnp.inf, f32), jnp.zeros((tq, 2 * HEAD_DIM), f32))
    m, acc = lax.fori_loop(0, (qi + 1) // 2, pair, init)

    def finish(acc):
        o = acc[:, :HEAD_DIM] / acc[:, HEAD_DIM:HEAD_DIM + 1]
        o_ref[0] = (o * gate_ref[0]).astype(bf16)

    @pl.when(qi % 2 == 1)
    def _():
        finish(acc)

    @pl.when(qi % 2 == 0)
    def _():
        s = jnp.where(col_minus_row <= 0, s_ref[...], -jnp.inf)
        finish(absorb(qi, s, m, acc)[1])


def _flash_prompt(q, k, v, c, gate, *, tq=FLASH_TILE):
    b, l, d = q.shape
    blk = lambda: pl.BlockSpec((1, tq, HEAD_DIM), lambda bi, h, qi: (bi, qi, h))
    seq = lambda: pl.BlockSpec((1, l, HEAD_DIM), lambda bi, h, qi: (bi, 0, h))
    return pl.pallas_call(
        functools.partial(_flash_kernel, tq=tq),
        grid=(b, N_HEADS, l // tq),
        in_specs=[blk(), seq(), seq(), pl.BlockSpec((1, l, N_HEADS), lambda bi, h, qi: (bi, 0, 0)), blk()],
        out_specs=blk(),
        out_shape=jax.ShapeDtypeStruct((b, l, d), bf16),
        scratch_shapes=[pltpu.VMEM((l, 2 * HEAD_DIM), bf16), pltpu.VMEM((l, 2 * HEAD_DIM), bf16),
                        pltpu.VMEM((l, HEAD_DIM), bf16), pltpu.VMEM((tq, tq), f32)],
        compiler_params=_params("parallel", "parallel", "arbitrary"),
        name="flash_prompt",
    )(q, k, v, c, gate)


T_PER_ROW = LANES // N_HEADS
SCORE_PAGES_PER_STEP = 16
PV_PAGES_PER_STEP = 16


def _eye(n, m):
    return lax.broadcasted_iota(jnp.int32, (n, m), 0) == lax.broadcasted_iota(jnp.int32, (n, m), 1)


def _lane_allreduce(x, op):
    sh = N_HEADS
    while sh < LANES:
        x = op(x, pltpu.roll(x, sh, 1))
        sh *= 2
    return x


def _flat_suffix(x):
    rows = x.shape[0]
    lane = lax.broadcasted_iota(jnp.int32, x.shape, 1)
    row = lax.broadcasted_iota(jnp.int32, x.shape, 0)
    y = x
    sh = N_HEADS
    while sh < LANES:
        y = y + jnp.where(lane < LANES - sh, pltpu.roll(y, LANES - sh, 1), 0.0)
        sh *= 2
    z = _lane_allreduce(jnp.where(lane < N_HEADS, y, 0.0), jnp.add)
    w = z
    sh = 1
    while sh < rows:
        w = w + jnp.where(row < rows - sh, pltpu.roll(w, rows - sh, 0), 0.0)
        sh *= 2
    return (y - x) + (w - z), w[0:1, :]


def _flat_scores(k3, q_rep):
    t = k3.shape[0]
    k2 = k3.reshape(t * N_HEADS, HEAD_DIM).astype(bf16)
    s = lax.dot_general(k2, q_rep, (((1,), (1,)), ((), ())), preferred_element_type=f32)
    s3 = s.reshape(t * N_HEADS // LANES, LANES, LANES)
    return jnp.sum(jnp.where(_eye(LANES, LANES)[None], s3, 0.0), axis=1)


def _dec_scores_kernel(pt_ref, q_ref, *refs, n_steps):
    g_pages = SCORE_PAGES_PER_STEP
    k_refs, lf_refs = refs[:g_pages], refs[g_pages:2 * g_pages]
    knew_ref, lfnew_ref, s_ref, snew_ref, carry_ref = refs[2 * g_pages:]
    p = pl.program_id(1)

    @pl.when(p == 0)
    def _():
        carry_ref[...] = jnp.zeros_like(carry_ref)

    q_rep = q_ref[0]
    carry = carry_ref[...]
    for g in range(g_pages):
        later, total = _flat_suffix(lf_refs[g][0, 0])
        s_ref[0, g_pages - 1 - g] = _flat_scores(k_refs[g][0, 0], q_rep) + (later + carry + lfnew_ref[0])
        carry = carry + total
    carry_ref[...] = carry

    @pl.when(p == n_steps - 1)
    def _():
        s_new = lax.dot_general(knew_ref[0].astype(bf16), q_rep, (((1,), (1,)), ((), ())),
                                preferred_element_type=f32)
        s_new = jnp.sum(jnp.where(_eye(N_HEADS, LANES), s_new, 0.0), axis=0, keepdims=True)
        row = lax.broadcasted_iota(jnp.int32, snew_ref.shape[1:], 0)
        lane = lax.broadcasted_iota(jnp.int32, snew_ref.shape[1:], 1)
        snew_ref[0] = jnp.where((row == 0) & (lane < N_HEADS), s_new, -jnp.inf)


def _dec_scores(page_table, q_rep, cache_k, lf_flat, layer, k_new, lf_new_rep):
    nb, n_pages = page_table.shape
    page = cache_k.shape[2]
    g_pages = SCORE_PAGES_PER_STEP
    n_steps = n_pages // g_pages
    rows = page // T_PER_ROW

    def past(g):
        return lambda b, p, pt: pt[b, n_pages - 1 - (p * g_pages + g)]

    k_specs = [pl.BlockSpec((1, 1, page, N_HEADS, HEAD_DIM),
                            lambda b, p, pt, f=past(g): (layer, f(b, p, pt), 0, 0, 0)) for g in range(g_pages)]
    lf_specs = [pl.BlockSpec((1, 1, rows, LANES),
                             lambda b, p, pt, f=past(g): (layer, f(b, p, pt), 0, 0)) for g in range(g_pages)]
    return pl.pallas_call(
        functools.partial(_dec_scores_kernel, n_steps=n_steps),
        grid_spec=pltpu.PrefetchScalarGridSpec(
            num_scalar_prefetch=1,
            grid=(nb, n_steps),
            in_specs=[pl.BlockSpec((1, LANES, HEAD_DIM), lambda b, p, pt: (b, 0, 0))] + k_specs + lf_specs + [
                pl.BlockSpec((1, N_HEADS, HEAD_DIM), lambda b, p, pt: (b, 0, 0)),
                pl.BlockSpec((1, 1, LANES), lambda b, p, pt: (b, 0, 0)),
            ],
            out_specs=[
                pl.BlockSpec((1, g_pages, rows, LANES), lambda b, p, pt: (b, n_steps - 1 - p, 0, 0)),
                pl.BlockSpec((1, SUBLANES, LANES), lambda b, p, pt: (b, 0, 0)),
            ],
            scratch_shapes=[pltpu.VMEM((1, LANES), f32)],
        ),
        out_shape=[jax.ShapeDtypeStruct((nb, n_pages, rows, LANES), f32),
                   jax.ShapeDtypeStruct((nb, SUBLANES, LANES), f32)],
        compiler_params=_params("parallel", "arbitrary"),
        name="dec_scores",
    )(page_table, q_rep, *([cache_k] * g_pages), *([lf_flat] * g_pages), k_new, lf_new_rep)


def _lane_broadcast_rows(flat):
    r = flat.shape[0]
    spread = jnp.where(_eye(LANES, LANES)[None], jnp.broadcast_to(flat[:, None, :], (r, LANES, LANES)), 0.0)
    ones = jnp.ones((LANES, HEAD_DIM), bf16)
    return jnp.dot(spread.reshape(r * LANES, LANES).astype(bf16), ones, preferred_element_type=f32)


def _dec_pv_kernel(pt_ref, s_ref, snew_ref, *refs, n_steps):
    g_pages = PV_PAGES_PER_STEP
    v_refs = refs[:g_pages]
    vnew_ref, gate_ref, o_ref, p_ref, pnew_ref, acc_ref = refs[g_pages:]
    p = pl.program_id(1)

    @pl.when(p == 0)
    def _():
        s = s_ref[0]
        s_new = snew_ref[0]
        m = jnp.maximum(jnp.max(jnp.max(s, axis=0), axis=0, keepdims=True), jnp.max(s_new, axis=0, keepdims=True))
        m = _lane_allreduce(m, jnp.maximum)
        e = jnp.exp(s - m[None])
        e_new = jnp.exp(s_new - m)
        denom = jnp.sum(jnp.sum(e, axis=0), axis=0, keepdims=True) + jnp.sum(e_new, axis=0, keepdims=True)
        denom = _lane_allreduce(denom, jnp.add)
        p_ref[...] = e / denom[None]
        pnew_ref[...] = e_new / denom
        acc_ref[...] = jnp.zeros_like(acc_ref)

    acc = acc_ref[...]
    for g in range(g_pages):
        v3 = v_refs[g][0, 0]
        pb = _lane_broadcast_rows(p_ref[p * g_pages + g])
        acc = acc + jnp.sum(pb.reshape(v3.shape) * v3, axis=0)
    acc_ref[...] = acc

    @pl.when(p == n_steps - 1)
    def _():
        p_new = jnp.where(_eye(N_HEADS, LANES), jnp.broadcast_to(pnew_ref[0:1, :], (N_HEADS, LANES)), 0.0)
        pb = jnp.dot(p_new.astype(bf16), jnp.ones((LANES, HEAD_DIM), bf16), preferred_element_type=f32)
        o_ref[0] = (acc + pb * vnew_ref[0]) * gate_ref[0]


def _dec_pv(page_table, s_all, s_new, cache_v, layer, v_new, gate):
    nb, n_pages = page_table.shape
    page = cache_v.shape[2]
    g_pages = PV_PAGES_PER_STEP
    n_steps = n_pages // g_pages
    rows = page // T_PER_ROW
    head_blk = lambda: pl.BlockSpec((1, N_HEADS, HEAD_DIM), lambda b, p, pt: (b, 0, 0))
    v_specs = [pl.BlockSpec((1, 1, page, N_HEADS, HEAD_DIM),
                            lambda b, p, pt, g=g: (layer, pt[b, p * g_pages + g], 0, 0, 0)) for g in range(g_pages)]
    return pl.pallas_call(
        functools.partial(_dec_pv_kernel, n_steps=n_steps),
        grid_spec=pltpu.PrefetchScalarGridSpec(
            num_scalar_prefetch=1,
            grid=(nb, n_steps),
            in_specs=[
                pl.BlockSpec((1, n_pages, rows, LANES), lambda b, p, pt: (b, 0, 0, 0)),
                pl.BlockSpec((1, SUBLANES, LANES), lambda b, p, pt: (b, 0, 0)),
            ] + v_specs + [head_blk(), head_blk()],
            out_specs=head_blk(),
            scratch_shapes=[pltpu.VMEM((n_pages, rows, LANES), f32), pltpu.VMEM((SUBLANES, LANES), f32),
                            pltpu.VMEM((N_HEADS, HEAD_DIM), f32)],
        ),
        out_shape=jax.ShapeDtypeStruct((nb, N_HEADS, HEAD_DIM), f32),
        compiler_params=_params("parallel", "arbitrary"),
        name="dec_pv",
    )(page_table, s_all, s_new, *([cache_v] * g_pages), v_new, gate)


def _final_norm_kernel(x_ref, g_ref, o_ref):
    o_ref[...] = _rms_rows(x_ref[...], g_ref[...])


def _final_norm(x, g, *, tm):
    m, d = x.shape
    return pl.pallas_call(
        _final_norm_kernel,
        grid=(m // tm,),
        in_specs=[pl.BlockSpec((tm, d), lambda i: (i, 0)), pl.BlockSpec((1, d), lambda i: (0, 0))],
        out_specs=pl.BlockSpec((tm, d), lambda i: (i, 0)),
        out_shape=jax.ShapeDtypeStruct((m, d), f32),
        compiler_params=_params("parallel"),
        name="final_norm",
    )(x, g.reshape(1, d))


def kernel(x_prompt, x_sample, cache_k, cache_v, cache_logf, page_table, norm_ffn1, ffn1_w_up, ffn1_w_down, norm_mix, norm_ffn2, ffn2_w_up, ffn2_w_down, gm_w_in, gm_ln_g, gm_ln_b, gm_w_s, gm_b_s, gm_w_out, fox_w_in, fox_b_f, fox_q_norm, fox_k_norm, fox_w_out, norm_final):
    bp, lp, d = x_prompt.shape
    bs, ls, _ = x_sample.shape
    assert ls == 1 and lp % FLASH_TILE == 0 and d == N_HEADS * HEAD_DIM
    depth = norm_ffn1.shape[0]
    mp, ms = bp * lp, bs * ls
    tm_p = min(ROW_TILE, mp)
    tm_down = min(ROW_TILE // 2, mp)
    half = gm_w_in.shape[2] // 2

    xp = x_prompt.reshape(mp, d)
    xs = x_sample.reshape(ms, d)
    lf_p, k_s, v_s, lf_s, gv_s = [], [], [], [], []
    kp_all = vp_all = None
    n_fox = fox_w_in.shape[0]
    n_pool, page = cache_logf.shape[1:3]
    assert page == LANES and page_table.shape[1] % max(SCORE_PAGES_PER_STEP, PV_PAGES_PER_STEP) == 0
    fox_w_t = jnp.swapaxes(fox_w_in, 1, 2)
    lf_flat = cache_logf.reshape(n_fox, n_pool, page // T_PER_ROW, LANES)

    for i in range(depth):
        xp, xs = _ffn(xp, xs, norm_ffn1[i], ffn1_w_up, ffn1_w_down, i, tm_up=tm_p, tm_down=tm_down)
        j = i // 2
        if i % 2 == 0:
            zp, zs = _gelu_proj(xp, xs, norm_mix[i], gm_w_in, j, tm=tm_p)
            gp = _gmlp_mix(zp, gm_ln_g[j], gm_ln_b[j], gm_w_s[j], gm_b_s[j].T)
            w00 = jnp.repeat(gm_w_s[j, :, 0, 0], GMLP_GROUP)
            b0 = jnp.repeat(gm_b_s[j, :, 0], GMLP_GROUP)
            gs, gv = _gmlp_mix_first_row(zs, gm_ln_g[j], gm_ln_b[j], w00, b0)
            gv_s.append(gv.reshape(bs, ls, half))
            xp, xs = _mm_res(gp, gs, gm_w_out, j, xp, xs, 1.0, tm=tm_p)
        else:
            w_ft = fox_w_t[j, 4 * d:, :]
            qp, kp_all, kp16, vp_all, vp16, gatep, qs, ks, vs, gates = _fox_proj(
                xp, xs, norm_mix[i], fox_w_t, j, fox_q_norm[j], fox_k_norm[j], kp_all, vp_all, n_fox,
                tm=tm_p, q_scale=SCORE_SCALE * LOG2E, qs_scale=SCORE_SCALE)
            lfp = _logf_proj(xp, norm_mix[i], w_ft, fox_b_f[j], tm=tm_p).reshape(bp, lp, N_HEADS)
            og = _flash_prompt(qp.reshape(bp, lp, d), kp16.reshape(bp, lp, d), vp16.reshape(bp, lp, d),
                               _cumsum_seq(lfp), gatep.reshape(bp, lp, d))
            lf_p.append(lfp)
            lfs = _logf_proj(xs, norm_mix[i], w_ft, fox_b_f[j], tm=ms)
            q_rep = jnp.tile(qs.reshape(bs, N_HEADS, HEAD_DIM), (1, T_PER_ROW, 1))
            lf_new_rep = jnp.tile(lfs, (1, T_PER_ROW)).reshape(bs, 1, LANES)
            ks3 = ks.reshape(bs, N_HEADS, HEAD_DIM)
            vs3 = vs.reshape(bs, N_HEADS, HEAD_DIM)
            k_s.append(ks3.reshape(bs, ls, N_HEADS, HEAD_DIM))
            v_s.append(vs3.reshape(bs, ls, N_HEADS, HEAD_DIM))
            s_all, s_new = _dec_scores(page_table, q_rep, cache_k, lf_flat, j, ks3, lf_new_rep)
            os_ = _dec_pv(page_table, s_all, s_new, cache_v, j, vs3, gates.reshape(bs, N_HEADS, HEAD_DIM))
            lf_s.append(lfs.reshape(bs, ls, N_HEADS))
            xp, xs = _mm_res(og.reshape(mp, d), os_.reshape(ms, d).astype(bf16), fox_w_out, j, xp, xs, 1.0, tm=tm_p)
        xp, xs = _ffn(xp, xs, norm_ffn2[i], ffn2_w_up, ffn2_w_down, i, tm_up=tm_p, tm_down=tm_down)

    y_prompt = _final_norm(xp, norm_final, tm=tm_down).reshape(bp, lp, d)
    y_sample = _final_norm(xs, norm_final, tm=ms).reshape(bs, ls, d)
    heads_p = (n_fox, bp, lp, N_HEADS, HEAD_DIM)
    return (y_prompt, y_sample, kp_all.reshape(heads_p), vp_all.reshape(heads_p), jnp.stack(lf_p),
            jnp.stack(k_s), jnp.stack(v_s), jnp.stack(lf_s), jnp.stack(gv_s))
```
